```python
import jax
import jax.numpy as jnp
from jax import lax
import numpy as np

D_MODEL = 2048
BATCH = 1
SEQ = 8192
DEPTH = 4

MLSTM_WIDTH = D_MODEL // 2
MLSTM_HEADS = 4
MLSTM_HEAD_DIM = MLSTM_WIDTH // MLSTM_HEADS
MLSTM_CHUNK = 64
MOBA_WIDTH = D_MODEL // 2
MOBA_HEADS = 8
MOBA_HEAD_DIM = MOBA_WIDTH // MOBA_HEADS
MOBA_BLOCK = 256
MOBA_TOPK = 3
MOBA_QCHUNK = 64
LRU_WIDTH = D_MODEL // 2
LRU_BLOCKS = 8
LRU_BLOCK_DIM = LRU_WIDTH // LRU_BLOCKS
LRU_CONV = 4
LRU_C = 8.0
D_FF = ((8 * D_MODEL // 3 + 127) // 128) * 128
EPS = 1e-6
IN_SPLITS = (MLSTM_WIDTH, MLSTM_WIDTH, MLSTM_WIDTH, MLSTM_WIDTH, MLSTM_HEADS, MLSTM_HEADS, MOBA_WIDTH, MOBA_WIDTH, MOBA_WIDTH, LRU_WIDTH, LRU_WIDTH, D_MODEL, D_MODEL, D_MODEL)
N_IN = sum(IN_SPLITS)

kernel_name = 'hybrid_mlstm_moba_rglru_macaron'


def rms_norm(x, g):
    xf = x.astype(jnp.float32)
    y = xf * lax.rsqrt(jnp.mean(xf * xf, axis=-1, keepdims=True) + EPS)
    return (y * g.astype(jnp.float32)).astype(x.dtype)


def swiglu(h, w_gate, w_up, w_down):
    return (jax.nn.silu(h @ w_gate) * (h @ w_up)) @ w_down


def to_heads(t, n_heads):
    b, s, w = t.shape
    return t.reshape(b, s, n_heads, w // n_heads).transpose(0, 2, 1, 3)


def from_heads(t):
    b, h, s, d = t.shape
    return t.transpose(0, 2, 1, 3).reshape(b, s, h * d)


def mlstm_chunkwise(q, k, v, i_pre, log_f):
    b, h, s, d = q.shape
    nc = s // MLSTM_CHUNK

    def to_chunks(t):
        return jnp.moveaxis(t.reshape(b, h, nc, MLSTM_CHUNK, *t.shape[3:]), 2, 0)

    causal = jnp.tril(jnp.ones((MLSTM_CHUNK, MLSTM_CHUNK), dtype=bool))

    def step(carry, inp):
        c_state, n_state, m_state = carry
        qc, kc, vc, ic, fc = inp
        bcum = jnp.cumsum(fc, axis=-1)
        d_intra = jnp.where(causal, bcum[..., :, None] - bcum[..., None, :] + ic[..., None, :], -jnp.inf)
        d_inter = bcum + m_state[..., None]
        m_t = jnp.maximum(jnp.max(d_intra, axis=-1), d_inter)
        w_intra = jnp.exp(d_intra - m_t[..., None])
        w_inter = jnp.exp(d_inter - m_t)
        sc = jnp.einsum('bhtd,bhsd->bhts', qc, kc) * w_intra
        num = jnp.einsum('bhts,bhsd->bhtd', sc, vc) + w_inter[..., None] * jnp.einsum('bhtk,bhvk->bhtv', qc, c_state)
        den = jnp.sum(sc, axis=-1) + w_inter * jnp.einsum('bhtk,bhk->bht', qc, n_state)
        h_out = num / jnp.maximum(jnp.abs(den), jnp.exp(-m_t))[..., None]
        b_last = bcum[..., -1]
        g = b_last[..., None] - bcum + ic
        m_new = jnp.maximum(b_last + m_state, jnp.max(g, axis=-1))
        decay = jnp.exp(b_last + m_state - m_new)
        w = jnp.exp(g - m_new[..., None])
        c_new = decay[..., None, None] * c_state + jnp.einsum('bhs,bhsv,bhsk->bhvk', w, vc, kc)
        n_new = decay[..., None] * n_state + jnp.einsum('bhs,bhsk->bhk', w, kc)
        return (c_new, n_new, m_new), h_out

    init = (jnp.zeros((b, h, d, d), jnp.float32), jnp.zeros((b, h, d), jnp.float32), jnp.zeros((b, h), jnp.float32))
    _, hs = lax.scan(step, init, (to_chunks(q), to_chunks(k), to_chunks(v), to_chunks(i_pre), to_chunks(log_f)))
    return jnp.moveaxis(hs, 0, 2).reshape(b, h, s, d)


def mlstm_branch(q, k, v, o_pre, i_pre, f_pre, b_i, b_f, head_gain):
    f32 = jnp.float32
    qh = to_heads(q.astype(f32), MLSTM_HEADS)
    kh = to_heads(k.astype(f32), MLSTM_HEADS) * (MLSTM_HEAD_DIM ** -0.5)
    vh = to_heads(v.astype(f32), MLSTM_HEADS)
    ig = (i_pre.astype(f32) + b_i.astype(f32)).transpose(0, 2, 1)
    lf = jax.nn.log_sigmoid(f_pre.astype(f32) + b_f.astype(f32)).transpose(0, 2, 1)
    hh = mlstm_chunkwise(qh, kh, vh, ig, lf)
    hh = hh * lax.rsqrt(jnp.mean(hh * hh, axis=-1, keepdims=True) + EPS)
    hh = from_heads(hh) * head_gain.astype(f32)
    return (jax.nn.sigmoid(o_pre.astype(f32)) * hh).astype(q.dtype)


def moba_attention(q, k, v):
    f32 = jnp.float32
    b, h, s, d = q.shape
    n_blk = -(-s // MOBA_BLOCK)
    s_pad = n_blk * MOBA_BLOCK
    pad = ((0, 0), (0, 0), (0, s_pad - s), (0, 0))
    q, k, v = jnp.pad(q, pad), jnp.pad(k, pad), jnp.pad(v, pad)
    kb = k.reshape(b, h, n_blk, MOBA_BLOCK, d)
    vb = v.reshape(b, h, n_blk, MOBA_BLOCK, d)
    k_mean = jnp.mean(kb.astype(f32), axis=3)
    q_blk = jnp.arange(s_pad) // MOBA_BLOCK
    past = jnp.arange(n_blk)[None, :] < q_blk[:, None]
    gate = jnp.einsum('bhtd,bhnd->bhtn', q.astype(f32), k_mean)
    gate = jnp.where(past, gate, -jnp.inf)
    top_k = min(MOBA_TOPK, n_blk)
    _, sel = lax.top_k(gate, top_k)
    sel_valid = sel < q_blk[None, None, :, None]
    scale = d ** -0.5
    bi = jnp.arange(b)[:, None, None, None]
    hi = jnp.arange(h)[None, :, None, None]

    def one_block(c):
        start = c * MOBA_QCHUNK
        qc = lax.dynamic_slice_in_dim(q, start, MOBA_QCHUNK, axis=2)
        sc = lax.dynamic_slice_in_dim(sel, start, MOBA_QCHUNK, axis=2)
        ok = lax.dynamic_slice_in_dim(sel_valid, start, MOBA_QCHUNK, axis=2)
        k_sel = kb[bi, hi, sc]
        v_sel = vb[bi, hi, sc]
        s_sel = jnp.einsum('bhqd,bhqnkd->bhqnk', qc, k_sel).astype(f32) * scale
        s_sel = jnp.where(ok[..., None], s_sel, -jnp.inf).reshape(b, h, MOBA_QCHUNK, top_k * MOBA_BLOCK)
        own = start // MOBA_BLOCK
        k_own = lax.dynamic_index_in_dim(kb, own, axis=2, keepdims=False)
        v_own = lax.dynamic_index_in_dim(vb, own, axis=2, keepdims=False)
        s_own = jnp.einsum('bhqd,bhkd->bhqk', qc, k_own).astype(f32) * scale
        q_pos = start + jnp.arange(MOBA_QCHUNK)
        k_pos = own * MOBA_BLOCK + jnp.arange(MOBA_BLOCK)
        s_own = jnp.where(k_pos[None, :] <= q_pos[:, None], s_own, -jnp.inf)
        p = jax.nn.softmax(jnp.concatenate([s_sel, s_own], axis=-1), axis=-1).astype(v.dtype)
        p_sel = p[..., :top_k * MOBA_BLOCK].reshape(b, h, MOBA_QCHUNK, top_k, MOBA_BLOCK)
        p_own = p[..., top_k * MOBA_BLOCK:]
        return jnp.einsum('bhqnk,bhqnkd->bhqd', p_sel, v_sel) + jnp.einsum('bhqk,bhkd->bhqd', p_own, v_own)

    out = lax.map(one_block, jnp.arange(s_pad // MOBA_QCHUNK))
    out = jnp.moveaxis(out, 0, 2).reshape(b, h, s_pad, d)
    return out[:, :, :s]


def moba_branch(q, k, v):
    return from_heads(moba_attention(to_heads(q, MOBA_HEADS), to_heads(k, MOBA_HEADS), to_heads(v, MOBA_HEADS)))


def linear_recurrence_combine(e1, e2):
    a1, b1 = e1
    a2, b2 = e2
    return a1 * a2, a2 * b1 + b2


def rglru_branch(xr, gate, conv_w, conv_b, w_a, b_a, w_x, b_x, lam):
    f32 = jnp.float32
    b, s, c = xr.shape
    xc = lax.conv_general_dilated(xr, conv_w[:, None, :], window_strides=(1,), padding=[(LRU_CONV - 1, 0)], dimension_numbers=('NWC', 'WIO', 'NWC'), feature_group_count=c) + conv_b

    def block_diag(t, w, bias):
        return jnp.einsum('bsnc,ncd->bsnd', t.reshape(b, s, LRU_BLOCKS, LRU_BLOCK_DIM), w).reshape(b, s, c) + bias

    r = jax.nn.sigmoid(block_diag(xc, w_a, b_a).astype(f32))
    i = jax.nn.sigmoid(block_diag(xc, w_x, b_x).astype(f32))
    log_a = -LRU_C * r * jax.nn.softplus(-lam.astype(f32))
    a = jnp.exp(log_a)
    u = jnp.sqrt(-jnp.expm1(2.0 * log_a)) * (i * xc.astype(f32))
    _, hseq = lax.associative_scan(linear_recurrence_combine, (a, u), axis=1)
    return (hseq * jax.nn.gelu(gate.astype(f32))).astype(xr.dtype)


def setup_inputs(seed: int = 0) -> dict:
    key = jax.random.key(seed)
    ks = iter(jax.random.split(key, 40))
    f32 = jnp.float32
    L = DEPTH

    def nrm(shape, fan_in):
        return jax.random.normal(next(ks), shape, f32) * (fan_in ** -0.5)

    def gain(shape):
        return 1.0 + 0.01 * jax.random.normal(next(ks), shape, f32)

    def small(shape, scale=0.01):
        return scale * jax.random.normal(next(ks), shape, f32)

    x = jax.random.normal(next(ks), (BATCH, SEQ, D_MODEL), f32)
    ffn1_norm = gain((L, D_MODEL))
    ffn1_w_gate = nrm((L, D_MODEL, D_FF), D_MODEL)
    ffn1_w_up = nrm((L, D_MODEL, D_FF), D_MODEL)
    ffn1_w_down = nrm((L, D_FF, D_MODEL), D_FF)
    mix_norm = gain((L, D_MODEL))
    w_in = nrm((L, D_MODEL, N_IN), D_MODEL)
    mlstm_b_i = small((L, MLSTM_HEADS), 0.1)
    mlstm_b_f = jnp.linspace(3.0, 6.0, MLSTM_HEADS, dtype=f32)[None, :] + small((L, MLSTM_HEADS), 0.1)
    mlstm_head_norm = gain((L, MLSTM_WIDTH))
    lru_conv_w = nrm((L, LRU_CONV, LRU_WIDTH), LRU_CONV)
    lru_conv_b = small((L, LRU_WIDTH))
    lru_w_a = nrm((L, LRU_BLOCKS, LRU_BLOCK_DIM, LRU_BLOCK_DIM), LRU_BLOCK_DIM)
    lru_b_a = small((L, LRU_WIDTH))
    lru_w_x = nrm((L, LRU_BLOCKS, LRU_BLOCK_DIM, LRU_BLOCK_DIM), LRU_BLOCK_DIM)
    lru_b_x = small((L, LRU_WIDTH))
    a0 = jax.random.uniform(next(ks), (L, LRU_WIDTH), f32, 0.9, 0.999)
    p0 = a0 ** (1.0 / LRU_C)
    lru_lambda = jnp.log(p0) - jnp.log1p(-p0)
    w_up_a = nrm((L, MLSTM_WIDTH, D_MODEL), MLSTM_WIDTH)
    w_up_b = nrm((L, MOBA_WIDTH, D_MODEL), MOBA_WIDTH)
    w_up_c = nrm((L, LRU_WIDTH, D_MODEL), LRU_WIDTH)
    w_out = nrm((L, D_MODEL, D_MODEL), D_MODEL)
    ffn2_norm = gain((L, D_MODEL))
    ffn2_w_gate = nrm((L, D_MODEL, D_FF), D_MODEL)
    ffn2_w_up = nrm((L, D_MODEL, D_FF), D_MODEL)
    ffn2_w_down = nrm((L, D_FF, D_MODEL), D_FF)
    final_norm = gain((D_MODEL,))
    return {'x': x, 'ffn1_norm': ffn1_norm, 'ffn1_w_gate': ffn1_w_gate, 'ffn1_w_up': ffn1_w_up, 'ffn1_w_down': ffn1_w_down,
            'mix_norm': mix_norm, 'w_in': w_in, 'mlstm_b_i': mlstm_b_i, 'mlstm_b_f': mlstm_b_f, 'mlstm_head_norm': mlstm_head_norm,
            'lru_conv_w': lru_conv_w, 'lru_conv_b': lru_conv_b, 'lru_w_a': lru_w_a, 'lru_b_a': lru_b_a, 'lru_w_x': lru_w_x,
            'lru_b_x': lru_b_x, 'lru_lambda': lru_lambda, 'w_up_a': w_up_a, 'w_up_b': w_up_b, 'w_up_c': w_up_c, 'w_out': w_out,
            'ffn2_norm': ffn2_norm, 'ffn2_w_gate': ffn2_w_gate, 'ffn2_w_up': ffn2_w_up, 'ffn2_w_down': ffn2_w_down,
            'final_norm': final_norm}


def reference(x, ffn1_norm, ffn1_w_gate, ffn1_w_up, ffn1_w_down, mix_norm, w_in, mlstm_b_i, mlstm_b_f, mlstm_head_norm,
              lru_conv_w, lru_conv_b, lru_w_a, lru_b_a, lru_w_x, lru_b_x, lru_lambda, w_up_a, w_up_b, w_up_c, w_out,
              ffn2_norm, ffn2_w_gate, ffn2_w_up, ffn2_w_down, final_norm):
    split_points = np.cumsum(np.array(IN_SPLITS))[:-1].tolist()
    for l in range(DEPTH):
        x = x + 0.5 * swiglu(rms_norm(x, ffn1_norm[l]), ffn1_w_gate[l], ffn1_w_up[l], ffn1_w_down[l])
        h = rms_norm(x, mix_norm[l])
        z = h @ w_in[l]
        (aq, ak, av, ao, ai, af, bq, bk, bv, cx, cg, ga, gb, gc) = jnp.split(z, split_points, axis=-1)
        y_a = mlstm_branch(aq, ak, av, ao, ai, af, mlstm_b_i[l], mlstm_b_f[l], mlstm_head_norm[l]) @ w_up_a[l]
        y_b = moba_branch(bq, bk, bv) @ w_up_b[l]
        y_c = rglru_branch(cx, cg, lru_conv_w[l], lru_conv_b[l], lru_w_a[l], lru_b_a[l], lru_w_x[l], lru_b_x[l], lru_lambda[l]) @ w_up_c[l]
        merged = jax.nn.sigmoid(ga) * y_a + jax.nn.sigmoid(gb) * y_b + jax.nn.sigmoid(gc) * y_c
        x = x + merged @ w_out[l]
        x = x + 0.5 * swiglu(rms_norm(x, ffn2_norm[l]), ffn2_w_gate[l], ffn2_w_up[l], ffn2_w_down[l])
    return rms_norm(x, final_norm)
```

```python
import functools

import jax
import jax.numpy as jnp
from jax import lax
from jax.experimental import pallas as pl
from jax.experimental.pallas import tpu as pltpu

F32 = jnp.float32
BF16 = jnp.bfloat16

D_MODEL = 2048
EPS = 1e-6
MLSTM_HEADS = 4
MLSTM_WIDTH = D_MODEL // 2
MLSTM_HEAD_DIM = MLSTM_WIDTH // MLSTM_HEADS
MOBA_HEADS = 8
MOBA_WIDTH = D_MODEL // 2
MOBA_HEAD_DIM = MOBA_WIDTH // MOBA_HEADS
MOBA_BLOCK = 256
MOBA_TOPK = 3
LRU_WIDTH = D_MODEL // 2
LRU_BLOCKS = 8
LRU_BLOCK_DIM = LRU_WIDTH // LRU_BLOCKS
LRU_CONV = 4
LRU_C = 8.0
D_FF = ((8 * D_MODEL // 3 + 127) // 128) * 128

LANES = 128
SUBLANES = 8
VMEM_LIMIT_BYTES = 48 * 1024 * 1024

ROW_TILE = 512
FF_TILE = 512
D_FF_PAD = ((D_FF + FF_TILE - 1) // FF_TILE) * FF_TILE
PROJ_COL_TILE = 1024
MERGE_COL_TILE = 512
MLSTM_CHUNK = 256
LRU_TILE = 512
GATE_PAD = LANES
GATE_ROWS = 16
MASK_NEG = -32768.0
NEG_INIT = -1e30

Z_AQ, Z_AV, Z_AO, Z_BQ, Z_BK, Z_BV, Z_CX, Z_CG, Z_GA, Z_GB, Z_GC = 0, 1, 2, 3, 4, 5, 6, 7, 8, 10, 12
Z_COLS = 14 * 1024

NT_DIMS = (((1,), (1,)), ((), ()))


def _params(*semantics):
    return pltpu.CompilerParams(dimension_semantics=semantics, vmem_limit_bytes=VMEM_LIMIT_BYTES)


def _rms_norm(x, g):
    return x * lax.rsqrt(jnp.mean(x * x, axis=-1, keepdims=True) + EPS) * g


def _sigmoid(x):
    return 1.0 / (1.0 + jnp.exp(-x))


def _log_sigmoid(x):
    return jnp.minimum(x, 0.0) - jnp.log1p(jnp.exp(-jnp.abs(x)))


def _softplus(x):
    return jnp.maximum(x, 0.0) + jnp.log1p(jnp.exp(-jnp.abs(x)))


def _split3(x):
    hi = x.astype(BF16)
    r1 = x - hi.astype(F32)
    mid = r1.astype(BF16)
    lo = (r1 - mid.astype(F32)).astype(BF16)
    return hi, mid, lo


def _ffn_body(x_ref, g_ref, wg_ref, wu_ref, wd_ref, *rest, final):
    if final:
        fg_ref, o_ref, h_s, acc_s = rest
    else:
        o_ref, h_s, acc_s = rest
    j = pl.program_id(1)

    @pl.when(j == 0)
    def _():
        h_s[...] = _rms_norm(x_ref[...], g_ref[...]).astype(BF16)

    h = h_s[...]
    a = jnp.dot(h, wg_ref[...], preferred_element_type=F32)
    b = jnp.dot(h, wu_ref[...], preferred_element_type=F32)
    act = (a * _sigmoid(a) * b).astype(BF16)
    contrib = jnp.dot(act, wd_ref[...], preferred_element_type=F32)

    @pl.when(j == 0)
    def _():
        acc_s[...] = contrib

    @pl.when(j > 0)
    def _():
        acc_s[...] += contrib

    @pl.when(j == pl.num_programs(1) - 1)
    def _():
        y = x_ref[...] + 0.5 * acc_s[...]
        if final:
            y = _rms_norm(y, fg_ref[...])
        o_ref[...] = y


def _ffn(x, norm_g, wg, wu, wd, final_g=None):
    s, d = x.shape
    f = wg.shape[1]
    tm = min(ROW_TILE, s)
    final = final_g is not None
    in_specs = [
        pl.BlockSpec((tm, d), lambda i, j: (i, 0)),
        pl.BlockSpec((1, d), lambda i, j: (0, 0)),
        pl.BlockSpec((d, FF_TILE), lambda i, j: (0, j)),
        pl.BlockSpec((d, FF_TILE), lambda i, j: (0, j)),
        pl.BlockSpec((FF_TILE, d), lambda i, j: (j, 0)),
    ]
    args = [x, norm_g, wg, wu, wd]
    if final:
        in_specs.append(pl.BlockSpec((1, d), lambda i, j: (0, 0)))
        args.append(final_g)
    return pl.pallas_call(
        functools.partial(_ffn_body, final=final),
        grid=(s // tm, f // FF_TILE),
        in_specs=in_specs,
        out_specs=pl.BlockSpec((tm, d), lambda i, j: (i, 0)),
        out_shape=jax.ShapeDtypeStruct((s, d), F32),
        scratch_shapes=[pltpu.VMEM((tm, d), BF16), pltpu.VMEM((tm, d), F32)],
        compiler_params=_params("parallel", "arbitrary"),
        name="ffn",
    )(*args)


def _proj_body(x_ref, g_ref, w_ref, wkt_ref, wif_ref, wift_ref, z_ref, kt_ref, ifc_ref, ifr_ref, h_s):
    @pl.when(pl.program_id(1) == 0)
    def _():
        h_s[...] = _rms_norm(x_ref[...], g_ref[...]).astype(BF16)
        h = h_s[...]
        kt_ref[...] = lax.dot_general(wkt_ref[...], h, NT_DIMS, preferred_element_type=F32).astype(BF16)
        ifc_ref[...] = jnp.dot(h, wif_ref[...], preferred_element_type=F32)
        ifr_ref[...] = lax.dot_general(wift_ref[...], h, NT_DIMS, preferred_element_type=F32)

    z_ref[...] = jnp.dot(h_s[...], w_ref[...], preferred_element_type=F32).astype(BF16)


def _proj(x, norm_g, w_main, w_kt, w_if, w_ift):
    s, d = x.shape
    tm = min(ROW_TILE, s)
    tn = PROJ_COL_TILE
    return pl.pallas_call(
        _proj_body,
        grid=(s // tm, Z_COLS // tn),
        in_specs=[
            pl.BlockSpec((tm, d), lambda i, j: (i, 0)),
            pl.BlockSpec((1, d), lambda i, j: (0, 0)),
            pl.BlockSpec((d, tn), lambda i, j: (0, j)),
            pl.BlockSpec((MLSTM_WIDTH, d), lambda i, j: (0, 0)),
            pl.BlockSpec((d, GATE_PAD), lambda i, j: (0, 0)),
            pl.BlockSpec((GATE_ROWS, d), lambda i, j: (0, 0)),
        ],
        out_specs=[
            pl.BlockSpec((tm, tn), lambda i, j: (i, j)),
            pl.BlockSpec((MLSTM_WIDTH, tm), lambda i, j: (0, i)),
            pl.BlockSpec((tm, GATE_PAD), lambda i, j: (i, 0)),
            pl.BlockSpec((GATE_ROWS, tm), lambda i, j: (0, i)),
        ],
        out_shape=[
            jax.ShapeDtypeStruct((s, Z_COLS), BF16),
            jax.ShapeDtypeStruct((MLSTM_WIDTH, s), BF16),
            jax.ShapeDtypeStruct((s, GATE_PAD), F32),
            jax.ShapeDtypeStruct((GATE_ROWS, s), F32),
        ],
        scratch_shapes=[pltpu.VMEM((tm, d), BF16)],
        compiler_params=_params("parallel", "arbitrary"),
        name="proj",
    )(x, norm_g, w_main, w_kt, w_if, w_ift)


def _mlstm_body(q_ref, v_ref, o_ref, kt_ref, ifc_ref, ifr_ref, bias_r_ref, bias_c_ref, gain_ref,
                y_ref, ct_s, m_s):
    chunk = q_ref.shape[0]
    dh = MLSTM_HEAD_DIM

    @pl.when(pl.program_id(0) == 0)
    def _():
        ct_s[...] = jnp.zeros_like(ct_s)
        m_s[...] = jnp.zeros_like(m_s)

    gates_c = ifc_ref[...] + bias_r_ref[...]
    gates_r = ifr_ref[...] + bias_c_ref[...]
    row = lax.broadcasted_iota(jnp.int32, (chunk, chunk), 0)
    col = lax.broadcasted_iota(jnp.int32, (chunk, chunk), 1)
    causal = row >= col
    tril = jnp.where(causal, 1.0, 0.0).astype(BF16)
    triu = jnp.where(row <= col, 1.0, 0.0).astype(BF16)
    bcum_c = sum(jnp.dot(tril, p, preferred_element_type=F32) for p in _split3(_log_sigmoid(gates_c)))
    bcum_r = sum(jnp.dot(p, triu, preferred_element_type=F32) for p in _split3(_log_sigmoid(gates_r)))
    ones_col = jnp.where(lax.broadcasted_iota(jnp.int32, (chunk, LANES), 1) == 0, 1.0, 0.0).astype(BF16)
    k_scale = dh ** -0.5

    for hd in range(MLSTM_HEADS):
        sl = slice(hd * dh, (hd + 1) * dh)
        fcol = MLSTM_HEADS + hd
        q = q_ref[:, sl]
        kt = kt_ref[sl, :]
        v_aug = jnp.concatenate([v_ref[:, sl], ones_col], axis=1)
        bc = bcum_c[:, fcol:fcol + 1]
        br = bcum_r[fcol:fcol + 1, :]
        i_r = gates_r[hd:hd + 1, :]
        m_prev = m_s[hd:hd + 1, 0:1]
        ct = ct_s[hd]

        d_intra = jnp.where(causal, bc - br + i_r, -jnp.inf)
        d_inter = bc + m_prev
        m_t = jnp.maximum(jnp.max(d_intra, axis=1, keepdims=True), d_inter)
        w_intra = jnp.exp(d_intra - m_t)
        w_inter = jnp.exp(d_inter - m_t)
        qk = jnp.dot(q, kt, preferred_element_type=F32)
        sc = (qk * k_scale * w_intra).astype(BF16)
        numden = (jnp.dot(sc, v_aug, preferred_element_type=F32)
                  + w_inter * jnp.dot(q, ct.astype(BF16), preferred_element_type=F32))
        num = numden[:, :dh]
        den = numden[:, dh:dh + 1]
        h_out = num / jnp.maximum(jnp.abs(den), jnp.exp(-m_t))
        h_out = h_out * lax.rsqrt(jnp.mean(h_out * h_out, axis=-1, keepdims=True) + EPS)
        h_out = h_out * gain_ref[:, sl]
        y_ref[:, sl] = (_sigmoid(o_ref[:, sl].astype(F32)) * h_out).astype(BF16)

        b_last = bc[chunk - 1:chunk, :]
        g_r = b_last - br + i_r
        m_new = jnp.maximum(b_last + m_prev, jnp.max(g_r, axis=1, keepdims=True))
        decay = jnp.exp(b_last + m_prev - m_new)
        w_r = jnp.exp(g_r - m_new) * k_scale
        ktw = (kt.astype(F32) * w_r).astype(BF16)
        ct_s[hd] = decay * ct + jnp.dot(ktw, v_aug, preferred_element_type=F32)
        m_s[hd:hd + 1, :] = jnp.broadcast_to(m_new, (1, LANES))


def _mlstm(z, kt, ifc, ifr, bias_r, bias_c, gain):
    s = z.shape[0]
    chunk = min(MLSTM_CHUNK, s)
    w = MLSTM_WIDTH
    return pl.pallas_call(
        _mlstm_body,
        grid=(s // chunk,),
        in_specs=[
            pl.BlockSpec((chunk, w), lambda c: (c, Z_AQ)),
            pl.BlockSpec((chunk, w), lambda c: (c, Z_AV)),
            pl.BlockSpec((chunk, w), lambda c: (c, Z_AO)),
            pl.BlockSpec((w, chunk), lambda c: (0, c)),
            pl.BlockSpec((chunk, GATE_PAD), lambda c: (c, 0)),
            pl.BlockSpec((GATE_ROWS, chunk), lambda c: (0, c)),
            pl.BlockSpec((1, GATE_PAD), lambda c: (0, 0)),
            pl.BlockSpec((GATE_ROWS, 1), lambda c: (0, 0)),
            pl.BlockSpec((1, w), lambda c: (0, 0)),
        ],
        out_specs=pl.BlockSpec((chunk, w), lambda c: (c, 0)),
        out_shape=jax.ShapeDtypeStruct((s, w), BF16),
        scratch_shapes=[
            pltpu.VMEM((MLSTM_HEADS, MLSTM_HEAD_DIM, MLSTM_HEAD_DIM + LANES), F32),
            pltpu.VMEM((SUBLANES, LANES), F32),
        ],
        compiler_params=_params("arbitrary"),
        name="mlstm",
    )(z, z, z, kt, ifc, ifr, bias_r, bias_c, gain)


def _moba_body(q_ref, k_ref, v_ref, y_ref, kmean_s):
    blk = MOBA_BLOCK
    dh = MOBA_HEAD_DIM
    n_blk = k_ref.shape[0] // blk
    i = pl.program_id(1)
    scale = dh ** -0.5

    @pl.when(i == 0)
    def _():
        kf = k_ref[...].astype(F32).reshape(n_blk, blk, dh)
        kmean_s[...] = jnp.zeros_like(kmean_s)
        kmean_s[0:n_blk, :] = jnp.mean(kf, axis=1)

    q = q_ref[...]
    km = kmean_s[...]
    km_hi = km.astype(BF16)
    km_lo = (km - km_hi.astype(F32)).astype(BF16)
    gate = (lax.dot_general(q, km_hi, NT_DIMS, preferred_element_type=F32)
            + lax.dot_general(q, km_lo, NT_DIMS, preferred_element_type=F32))
    lane = lax.broadcasted_iota(jnp.int32, (blk, LANES), 1).astype(F32)
    g = jnp.where(lane < i.astype(F32), gate, -jnp.inf)
    bias = jnp.full((blk, LANES), MASK_NEG, F32)
    for _ in range(MOBA_TOPK):
        mx = jnp.max(g, axis=1, keepdims=True)
        first = jnp.min(jnp.where(g == mx, lane, float(LANES)), axis=1, keepdims=True)
        first = jnp.where(mx > -jnp.inf, first, -1.0)
        pick = lane == first
        bias = jnp.where(pick, 0.0, bias)
        g = jnp.where(pick, -jnp.inf, g)
    q_aug = jnp.concatenate([q, bias.astype(BF16)], axis=1)

    def attend(s, v, m, l, acc):
        m_new = jnp.maximum(m, jnp.max(s, axis=1, keepdims=True))
        alpha = jnp.exp(m - m_new)
        p = jnp.exp(s - m_new)
        l_new = alpha * l + jnp.sum(p, axis=1, keepdims=True)
        acc_new = alpha * acc + jnp.dot(p.astype(BF16), v, preferred_element_type=F32)
        return m_new, l_new, acc_new

    lane_k = lax.broadcasted_iota(jnp.int32, (blk, LANES), 1)

    def past_block(j, carry):
        start = pl.multiple_of(j * blk, blk)
        k_j = k_ref[pl.ds(start, blk), :]
        v_j = v_ref[pl.ds(start, blk), :]
        onehot = jnp.where(lane_k == j, 1.0, 0.0).astype(BF16)
        k_aug = jnp.concatenate([k_j, onehot], axis=1)
        s = lax.dot_general(q_aug, k_aug, NT_DIMS, preferred_element_type=F32) * scale
        return attend(s, v_j, *carry)

    init = (jnp.full((blk, 1), NEG_INIT, F32), jnp.zeros((blk, 1), F32), jnp.zeros((blk, dh), F32))
    carry = lax.fori_loop(0, i, past_block, init)

    start = pl.multiple_of(i * blk, blk)
    k_i = k_ref[pl.ds(start, blk), :]
    v_i = v_ref[pl.ds(start, blk), :]
    s = lax.dot_general(q, k_i, NT_DIMS, preferred_element_type=F32) * scale
    row = lax.broadcasted_iota(jnp.int32, (blk, blk), 0)
    col = lax.broadcasted_iota(jnp.int32, (blk, blk), 1)
    s = jnp.where(col <= row, s, -jnp.inf)
    _, l, acc = attend(s, v_i, *carry)
    y_ref[...] = (acc / l).astype(BF16)


def _moba(z):
    s = z.shape[0]
    blk = MOBA_BLOCK
    dh = MOBA_HEAD_DIM
    per_tile = MOBA_WIDTH // dh
    return pl.pallas_call(
        _moba_body,
        grid=(MOBA_HEADS, s // blk),
        in_specs=[
            pl.BlockSpec((blk, dh), lambda h, i: (i, Z_BQ * per_tile + h)),
            pl.BlockSpec((s, dh), lambda h, i: (0, Z_BK * per_tile + h)),
            pl.BlockSpec((s, dh), lambda h, i: (0, Z_BV * per_tile + h)),
        ],
        out_specs=pl.BlockSpec((blk, dh), lambda h, i: (i, h)),
        out_shape=jax.ShapeDtypeStruct((s, MOBA_WIDTH), BF16),
        scratch_shapes=[pltpu.VMEM((LANES, dh), F32)],
        compiler_params=_params("parallel", "arbitrary"),
        name="moba",
    )(z, z, z)


def _lru_body(x_ref, gate_ref, cw_ref, cb_ref, wa_ref, ba_ref, wx_ref, bx_ref, lam_ref,
              y_ref, tail_s, hlast_s, a_s, u_s):
    t, c = x_ref.shape

    @pl.when(pl.program_id(0) == 0)
    def _():
        tail_s[...] = jnp.zeros_like(tail_s)
        hlast_s[...] = jnp.zeros_like(hlast_s)

    x = x_ref[...].astype(F32)
    xx = jnp.concatenate([tail_s[...], x], axis=0)
    tail_s[...] = x[t - SUBLANES:, :]
    xc = cb_ref[...] + cw_ref[LRU_CONV - 1:LRU_CONV, :] * x
    for tap in range(LRU_CONV - 1):
        shift = LRU_CONV - 1 - tap
        xc = xc + cw_ref[tap:tap + 1, :] * xx[SUBLANES - shift:SUBLANES - shift + t, :]

    xcb = xc.astype(BF16)
    bd = LRU_BLOCK_DIM
    ra = jnp.concatenate(
        [jnp.dot(xcb[:, n * bd:(n + 1) * bd], wa_ref[n], preferred_element_type=F32) for n in range(LRU_BLOCKS)],
        axis=1)
    rx = jnp.concatenate(
        [jnp.dot(xcb[:, n * bd:(n + 1) * bd], wx_ref[n], preferred_element_type=F32) for n in range(LRU_BLOCKS)],
        axis=1)
    r = _sigmoid(ra + ba_ref[...])
    gi = _sigmoid(rx + bx_ref[...])
    log_a = -LRU_C * r * _softplus(-lam_ref[...])
    a = jnp.exp(log_a)
    a_s[...] = a
    u_s[...] = jnp.sqrt(-jnp.tanh(log_a) * (a * a + 1.0)) * (gi * xc)

    sub = lax.broadcasted_iota(jnp.int32, (SUBLANES, c), 0)

    def group(gidx, h_prev):
        r0 = pl.multiple_of(gidx * SUBLANES, SUBLANES)
        a = a_s[pl.ds(r0, SUBLANES), :]
        u = u_s[pl.ds(r0, SUBLANES), :]
        d = 1
        while d < SUBLANES:
            keep = sub >= d
            a_sh = jnp.where(keep, pltpu.roll(a, d, 0), 1.0)
            u_sh = jnp.where(keep, pltpu.roll(u, d, 0), 0.0)
            u = a * u_sh + u
            a = a * a_sh
            d *= 2
        h = u + a * h_prev
        u_s[pl.ds(r0, SUBLANES), :] = h
        return jnp.broadcast_to(h[SUBLANES - 1:SUBLANES, :], (SUBLANES, c))

    h_last = lax.fori_loop(0, t // SUBLANES, group, hlast_s[...], unroll=4)
    hlast_s[...] = h_last

    gt = gate_ref[...].astype(F32)
    gelu = 0.5 * gt * (1.0 + jnp.tanh(0.7978845608028654 * (gt + 0.044715 * (gt * gt * gt))))
    y_ref[...] = (u_s[...] * gelu).astype(BF16)


def _lru(z, conv_w, conv_b, w_a, b_a, w_x, b_x, lam):
    s = z.shape[0]
    c = LRU_WIDTH
    t = min(LRU_TILE, s)
    bd = LRU_BLOCK_DIM
    vec = pl.BlockSpec((1, c), lambda i: (0, 0))
    blockw = pl.BlockSpec((LRU_BLOCKS, bd, bd), lambda i: (0, 0, 0))
    return pl.pallas_call(
        _lru_body,
        grid=(s // t,),
        in_specs=[
            pl.BlockSpec((t, c), lambda i: (i, Z_CX)),
            pl.BlockSpec((t, c), lambda i: (i, Z_CG)),
            pl.BlockSpec((LRU_CONV, c), lambda i: (0, 0)),
            vec, blockw, vec, blockw, vec, vec,
        ],
        out_specs=pl.BlockSpec((t, c), lambda i: (i, 0)),
        out_shape=jax.ShapeDtypeStruct((s, c), BF16),
        scratch_shapes=[
            pltpu.VMEM((SUBLANES, c), F32),
            pltpu.VMEM((SUBLANES, c), F32),
            pltpu.VMEM((t, c), F32),
            pltpu.VMEM((t, c), F32),
        ],
        compiler_params=_params("arbitrary"),
        name="rglru",
    )(z, z, conv_w, conv_b, w_a, b_a, w_x, b_x, lam)


def _merge_body(x_ref, ya_ref, yb_ref, yc_ref, ga_ref, gb_ref, gc_ref, wa_ref, wb_ref, wc_ref, wo_ref,
                o_ref, acc_s):
    j = pl.program_id(1)
    merged = (_sigmoid(ga_ref[...].astype(F32)) * jnp.dot(ya_ref[...], wa_ref[...], preferred_element_type=F32)
              + _sigmoid(gb_ref[...].astype(F32)) * jnp.dot(yb_ref[...], wb_ref[...], preferred_element_type=F32)
              + _sigmoid(gc_ref[...].astype(F32)) * jnp.dot(yc_ref[...], wc_ref[...], preferred_element_type=F32))
    contrib = jnp.dot(merged.astype(BF16), wo_ref[...], preferred_element_type=F32)

    @pl.when(j == 0)
    def _():
        acc_s[...] = contrib

    @pl.when(j > 0)
    def _():
        acc_s[...] += contrib

    @pl.when(j == pl.num_programs(1) - 1)
    def _():
        o_ref[...] = x_ref[...] + acc_s[...]


def _merge(x, ya, yb, yc, z, wa, wb, wc, wo):
    s, d = x.shape
    tm = min(ROW_TILE, s)
    tn = MERGE_COL_TILE
    w = ya.shape[1]
    per_tile = 1024 // tn
    ybranch = pl.BlockSpec((tm, w), lambda i, j: (i, 0))
    wup = pl.BlockSpec((w, tn), lambda i, j: (0, j))

    def gate_spec(base):
        return pl.BlockSpec((tm, tn), lambda i, j: (i, base * per_tile + j))

    return pl.pallas_call(
        _merge_body,
        grid=(s // tm, d // tn),
        in_specs=[
            pl.BlockSpec((tm, d), lambda i, j: (i, 0)),
            ybranch, ybranch, ybranch,
            gate_spec(Z_GA), gate_spec(Z_GB), gate_spec(Z_GC),
            wup, wup, wup,
            pl.BlockSpec((tn, d), lambda i, j: (j, 0)),
        ],
        out_specs=pl.BlockSpec((tm, d), lambda i, j: (i, 0)),
        out_shape=jax.ShapeDtypeStruct((s, d), F32),
        scratch_shapes=[pltpu.VMEM((tm, d), F32)],
        compiler_params=_params("parallel", "arbitrary"),
        name="merge",
    )(x, ya, yb, yc, z, z, z, wa, wb, wc, wo)


def _pad_ff(w, axis):
    pad = [(0, 0), (0, 0)]
    pad[axis] = (0, D_FF_PAD - D_FF)
    return jnp.pad(w.astype(BF16), pad)


def _row(v):
    return v.reshape(1, -1).astype(F32)


def _proj_weights(w_in):
    w = MLSTM_WIDTH
    hds = MLSTM_HEADS
    aq, ak, av, ao = (w_in[:, n * w:(n + 1) * w] for n in range(4))
    gates = w_in[:, 4 * w:4 * w + 2 * hds]
    rest = w_in[:, 4 * w + 2 * hds:]
    w_main = jnp.concatenate([aq, av, ao, rest], axis=1).astype(BF16)
    w_kt = ak.T.astype(BF16)
    w_if = jnp.pad(gates, ((0, 0), (0, GATE_PAD - 2 * hds))).astype(BF16)
    w_ift = jnp.pad(gates.T, ((0, GATE_ROWS - 2 * hds), (0, 0))).astype(BF16)
    return w_main, w_kt, w_if, w_ift


def kernel(x, ffn1_norm, ffn1_w_gate, ffn1_w_up, ffn1_w_down, mix_norm, w_in, mlstm_b_i, mlstm_b_f, mlstm_head_norm, lru_conv_w, lru_conv_b, lru_w_a, lru_b_a, lru_w_x, lru_b_x, lru_lambda, w_up_a, w_up_b, w_up_c, w_out, ffn2_norm, ffn2_w_gate, ffn2_w_up, ffn2_w_down, final_norm):
    b, s, d = x.shape
    depth = w_in.shape[0]
    outs = []
    for bi in range(b):
        xs = x[bi]
        for l in range(depth):
            xs = _ffn(xs, _row(ffn1_norm[l]), _pad_ff(ffn1_w_gate[l], 1), _pad_ff(ffn1_w_up[l], 1),
                      _pad_ff(ffn1_w_down[l], 0))
            z, kt, ifc, ifr = _proj(xs, _row(mix_norm[l]), *_proj_weights(w_in[l]))
            gate_bias = jnp.concatenate([mlstm_b_i[l], mlstm_b_f[l]]).astype(F32)
            bias_r = jnp.pad(gate_bias, (0, GATE_PAD - gate_bias.shape[0])).reshape(1, GATE_PAD)
            bias_c = jnp.pad(gate_bias, (0, GATE_ROWS - gate_bias.shape[0])).reshape(GATE_ROWS, 1)
            ya = _mlstm(z, kt, ifc, ifr, bias_r, bias_c, _row(mlstm_head_norm[l]))
            yb = _moba(z)
            yc = _lru(z, lru_conv_w[l].astype(F32), _row(lru_conv_b[l]), lru_w_a[l].astype(BF16), _row(lru_b_a[l]),
                      lru_w_x[l].astype(BF16), _row(lru_b_x[l]), _row(lru_lambda[l]))
            xs = _merge(xs, ya, yb, yc, z, w_up_a[l].astype(BF16), w_up_b[l].astype(BF16),
                        w_up_c[l].astype(BF16), w_out[l].astype(BF16))
            xs = _ffn(xs, _row(ffn2_norm[l]), _pad_ff(ffn2_w_gate[l], 1), _pad_ff(ffn2_w_up[l], 1),
                      _pad_ff(ffn2_w_down[l], 0), final_g=_row(final_norm) if l == depth - 1 else None)
        outs.append(xs)
    return jnp.stack(outs).astype(x.dtype)
```

```python
import functools
import math

import jax
import jax.numpy as jnp
from jax import lax
from jax.experimental import pallas as pl
from jax.experimental.pallas import tpu as pltpu

F32 = jnp.float32
BF16 = jnp.bfloat16

D_MODEL = 2048
EPS = 1e-6
MLSTM_HEADS = 4
MLSTM_WIDTH = D_MODEL // 2
MLSTM_HEAD_DIM = MLSTM_WIDTH // MLSTM_HEADS
MOBA_HEADS = 8
MOBA_WIDTH = D_MODEL // 2
MOBA_HEAD_DIM = MOBA_WIDTH // MOBA_HEADS
MOBA_BLOCK = 256
MOBA_TOPK = 3
LRU_WIDTH = D_MODEL // 2
LRU_BLOCKS = 8
LRU_BLOCK_DIM = LRU_WIDTH // LRU_BLOCKS
LRU_CONV = 4
LRU_C = 8.0
D_FF = ((8 * D_MODEL // 3 + 127) // 128) * 128

LANES = 128
SUBLANES = 8
MIB = 1024 * 1024

ROW_TILE = 512
FFN_ROW_TILE = 1024
FF_TILE = 256
PROJ_COL_TILE = 2048
MERGE_COL_TILE = 512
MLSTM_CHUNK = 256
LRU_TILE = 512
MOBA_HEAD_GROUP = 8
GATE_PAD = LANES
GATE_ROWS = 16
MASK_NEG = -1e30
NEG_INIT = -1e30
MOBA_Q_SCALE = MOBA_HEAD_DIM ** -0.5 * math.log2(math.e)

Z_AQ, Z_AV, Z_AO, Z_BK, Z_CX, Z_CG, Z_GA, Z_GB, Z_GC = 0, 1, 2, 3, 4, 5, 6, 8, 10
Z_COLS = 12 * 1024

NT_DIMS = (((1,), (1,)), ((), ()))


def _params(vmem_mib, *semantics):
    return pltpu.CompilerParams(dimension_semantics=semantics, vmem_limit_bytes=vmem_mib * MIB)


def _resident(block_shape, index_map):
    return pl.BlockSpec(block_shape, index_map, pipeline_mode=pl.Buffered(1))


def _rms_norm(x, g):
    return x * lax.rsqrt(jnp.mean(x * x, axis=-1, keepdims=True) + EPS) * g


def _sigmoid(x):
    return 1.0 / (1.0 + jnp.exp(-x))


def _log_sigmoid(x):
    return jnp.minimum(x, 0.0) - jnp.log1p(jnp.exp(-jnp.abs(x)))


def _softplus(x):
    return jnp.maximum(x, 0.0) + jnp.log1p(jnp.exp(-jnp.abs(x)))


def _split3(x):
    hi = x.astype(BF16)
    r1 = x - hi.astype(F32)
    mid = r1.astype(BF16)
    lo = (r1 - mid.astype(F32)).astype(BF16)
    return hi, mid, lo


def _ffn_body(x_ref, g_ref, wg_ref, wu_ref, wd_ref, *rest, final):
    if final:
        fg_ref, o_ref, h_s = rest
    else:
        o_ref, h_s = rest
    j = pl.program_id(1)
    tf = wg_ref.shape[1]

    @pl.when(j == 0)
    def _():
        x = x_ref[...]
        h_s[...] = _rms_norm(x, g_ref[...]).astype(BF16)
        o_ref[...] = x

    h = h_s[...]
    valid = D_FF - j * tf
    a = jnp.dot(h, wg_ref[...].astype(BF16), preferred_element_type=F32)
    b = jnp.dot(h, wu_ref[...].astype(BF16), preferred_element_type=F32)
    col = lax.broadcasted_iota(jnp.int32, (1, tf), 1)
    act = jnp.where(col < valid, 0.5 * (a * _sigmoid(a) * b), 0.0).astype(BF16)
    row = lax.broadcasted_iota(jnp.int32, (tf, 1), 0)
    wd = jnp.where(row < valid, wd_ref[...], 0.0).astype(BF16)
    o_ref[...] += jnp.dot(act, wd, preferred_element_type=F32)

    if final:
        @pl.when(j == pl.num_programs(1) - 1)
        def _():
            o_ref[...] = _rms_norm(o_ref[...], fg_ref[...])


def _ffn(x, norm_g, wg, wu, wd, layer, final_g=None):
    s, d = x.shape
    tm = min(FFN_ROW_TILE, s)
    tf = FF_TILE
    final = final_g is not None
    in_specs = [
        _resident((tm, d), lambda i, j: (i, 0)),
        pl.BlockSpec((1, d), lambda i, j: (0, 0)),
        pl.BlockSpec((None, d, tf), lambda i, j: (layer, 0, j)),
        pl.BlockSpec((None, d, tf), lambda i, j: (layer, 0, j)),
        pl.BlockSpec((None, tf, d), lambda i, j: (layer, j, 0)),
    ]
    args = [x, norm_g, wg, wu, wd]
    if final:
        in_specs.append(pl.BlockSpec((1, d), lambda i, j: (0, 0)))
        args.append(final_g)
    return pl.pallas_call(
        functools.partial(_ffn_body, final=final),
        grid=(s // tm, pl.cdiv(D_FF, tf)),
        in_specs=in_specs,
        out_specs=pl.BlockSpec((tm, d), lambda i, j: (i, 0)),
        out_shape=jax.ShapeDtypeStruct((s, d), F32),
        scratch_shapes=[pltpu.VMEM((tm, d), BF16)],
        compiler_params=_params(56, "parallel", "arbitrary"),
        name="ffn",
    )(*args)


def _proj_body(x_ref, g_ref, w_ref, wt_ref, wif_ref, wift_ref,
               z_ref, akt_ref, bqt_ref, bvt_ref, ifc_ref, ifr_ref, h_s):
    @pl.when(pl.program_id(1) == 0)
    def _():
        h_s[...] = _rms_norm(x_ref[...], g_ref[...]).astype(BF16)
        h = h_s[...]
        w = MLSTM_WIDTH
        akt_ref[...] = jnp.dot(h, wt_ref[:, 0:w], preferred_element_type=F32).astype(BF16).T
        bq = jnp.dot(h, wt_ref[:, w:2 * w], preferred_element_type=F32) * MOBA_Q_SCALE
        bqt_ref[...] = bq.astype(BF16).T
        bvt = jnp.dot(h, wt_ref[:, 2 * w:3 * w], preferred_element_type=F32).astype(BF16).T
        for n in range(bvt_ref.shape[0]):
            bvt_ref[n] = bvt[:, n * MOBA_BLOCK:(n + 1) * MOBA_BLOCK]
        ifc_ref[...] = jnp.dot(h, wif_ref[...], preferred_element_type=F32)
        ifr_ref[...] = lax.dot_general(wift_ref[...], h, NT_DIMS, preferred_element_type=F32)

    z_ref[...] = jnp.dot(h_s[...], w_ref[...], preferred_element_type=F32).astype(BF16)


def _proj(x, norm_g, w_main, w_t, w_if, w_ift):
    s, d = x.shape
    tm = min(ROW_TILE, s)
    tn = PROJ_COL_TILE
    blk = MOBA_BLOCK
    return pl.pallas_call(
        _proj_body,
        grid=(s // tm, Z_COLS // tn),
        in_specs=[
            _resident((tm, d), lambda i, j: (i, 0)),
            pl.BlockSpec((1, d), lambda i, j: (0, 0)),
            pl.BlockSpec((d, tn), lambda i, j: (0, j)),
            _resident((d, 3 * MLSTM_WIDTH), lambda i, j: (0, 0)),
            pl.BlockSpec((d, GATE_PAD), lambda i, j: (0, 0)),
            pl.BlockSpec((GATE_ROWS, d), lambda i, j: (0, 0)),
        ],
        out_specs=[
            pl.BlockSpec((tm, tn), lambda i, j: (i, j)),
            pl.BlockSpec((MLSTM_WIDTH, tm), lambda i, j: (0, i)),
            pl.BlockSpec((MOBA_WIDTH, tm), lambda i, j: (0, i)),
            pl.BlockSpec((tm // blk, MOBA_WIDTH, blk), lambda i, j: (i, 0, 0)),
            pl.BlockSpec((tm, GATE_PAD), lambda i, j: (i, 0)),
            pl.BlockSpec((GATE_ROWS, tm), lambda i, j: (0, i)),
        ],
        out_shape=[
            jax.ShapeDtypeStruct((s, Z_COLS), BF16),
            jax.ShapeDtypeStruct((MLSTM_WIDTH, s), BF16),
            jax.ShapeDtypeStruct((MOBA_WIDTH, s), BF16),
            jax.ShapeDtypeStruct((s // blk, MOBA_WIDTH, blk), BF16),
            jax.ShapeDtypeStruct((s, GATE_PAD), F32),
            jax.ShapeDtypeStruct((GATE_ROWS, s), F32),
        ],
        scratch_shapes=[pltpu.VMEM((tm, d), BF16)],
        compiler_params=_params(56, "parallel", "arbitrary"),
        name="proj",
    )(x, norm_g, w_main, w_t, w_if, w_ift)


def _mlstm_body(q_ref, v_ref, o_ref, kt_ref, ifc_ref, ifr_ref, bias_r_ref, bias_c_ref, gain_ref,
                y_ref, ct_s, m_s):
    chunk = q_ref.shape[0]
    dh = MLSTM_HEAD_DIM

    @pl.when(pl.program_id(0) == 0)
    def _():
        ct_s[...] = jnp.zeros_like(ct_s)
        m_s[...] = jnp.zeros_like(m_s)

    gates_c = ifc_ref[...] + bias_r_ref[...]
    gates_r = ifr_ref[...] + bias_c_ref[...]
    row = lax.broadcasted_iota(jnp.int32, (chunk, chunk), 0)
    col = lax.broadcasted_iota(jnp.int32, (chunk, chunk), 1)
    causal = row >= col
    tril = jnp.where(causal, 1.0, 0.0).astype(BF16)
    triu = jnp.where(row <= col, 1.0, 0.0).astype(BF16)
    bcum_c = sum(jnp.dot(tril, p, preferred_element_type=F32) for p in _split3(_log_sigmoid(gates_c)))
    bcum_r = sum(jnp.dot(p, triu, preferred_element_type=F32) for p in _split3(_log_sigmoid(gates_r)))
    ones_col = jnp.where(lax.broadcasted_iota(jnp.int32, (chunk, LANES), 1) == 0, 1.0, 0.0).astype(BF16)
    k_scale = dh ** -0.5

    for hd in range(MLSTM_HEADS):
        sl = slice(hd * dh, (hd + 1) * dh)
        fcol = MLSTM_HEADS + hd
        q = q_ref[:, sl]
        kt = kt_ref[sl, :]
        v_aug = jnp.concatenate([v_ref[:, sl], ones_col], axis=1)
        bc = bcum_c[:, fcol:fcol + 1]
        br = bcum_r[fcol:fcol + 1, :]
        i_r = gates_r[hd:hd + 1, :]
        m_prev = m_s[hd:hd + 1, 0:1]
        ct = ct_s[hd]

        d_intra = jnp.where(causal, bc - br + i_r, -jnp.inf)
        d_inter = bc + m_prev
        m_t = jnp.maximum(jnp.max(d_intra, axis=1, keepdims=True), d_inter)
        w_intra = jnp.exp(d_intra - m_t)
        w_inter = jnp.exp(d_inter - m_t)
        qk = jnp.dot(q, kt, preferred_element_type=F32)
        sc = (qk * k_scale * w_intra).astype(BF16)
        numden = (jnp.dot(sc, v_aug, preferred_element_type=F32)
                  + w_inter * jnp.dot(q, ct.astype(BF16), preferred_element_type=F32))
        num = numden[:, :dh]
        den = numden[:, dh:dh + 1]
        h_out = num / jnp.maximum(jnp.abs(den), jnp.exp(-m_t))
        h_out = h_out * lax.rsqrt(jnp.mean(h_out * h_out, axis=-1, keepdims=True) + EPS)
        h_out = h_out * gain_ref[:, sl]
        y_ref[:, sl] = (_sigmoid(o_ref[:, sl].astype(F32)) * h_out).astype(BF16)

        b_last = bc[chunk - 1:chunk, :]
        g_r = b_last - br + i_r
        m_new = jnp.maximum(b_last + m_prev, jnp.max(g_r, axis=1, keepdims=True))
        decay = jnp.exp(b_last + m_prev - m_new)
        w_r = jnp.exp(g_r - m_new) * k_scale
        ktw = (kt.astype(F32) * w_r).astype(BF16)
        ct_s[hd] = decay * ct + jnp.dot(ktw, v_aug, preferred_element_type=F32)
        m_s[hd:hd + 1, :] = jnp.broadcast_to(m_new, (1, LANES))


def _mlstm(z, kt, ifc, ifr, bias_r, bias_c, gain):
    s = z.shape[0]
    chunk = min(MLSTM_CHUNK, s)
    w = MLSTM_WIDTH
    return pl.pallas_call(
        _mlstm_body,
        grid=(s // chunk,),
        in_specs=[
            pl.BlockSpec((chunk, w), lambda c: (c, Z_AQ)),
            pl.BlockSpec((chunk, w), lambda c: (c, Z_AV)),
            pl.BlockSpec((chunk, w), lambda c: (c, Z_AO)),
            pl.BlockSpec((w, chunk), lambda c: (0, c)),
            pl.BlockSpec((chunk, GATE_PAD), lambda c: (c, 0)),
            pl.BlockSpec((GATE_ROWS, chunk), lambda c: (0, c)),
            pl.BlockSpec((1, GATE_PAD), lambda c: (0, 0)),
            pl.BlockSpec((GATE_ROWS, 1), lambda c: (0, 0)),
            pl.BlockSpec((1, w), lambda c: (0, 0)),
        ],
        out_specs=pl.BlockSpec((chunk, w), lambda c: (c, 0)),
        out_shape=jax.ShapeDtypeStruct((s, w), BF16),
        scratch_shapes=[
            pltpu.VMEM((MLSTM_HEADS, MLSTM_HEAD_DIM, MLSTM_HEAD_DIM + LANES), F32),
            pltpu.VMEM((SUBLANES, LANES), F32),
        ],
        compiler_params=_params(32, "arbitrary"),
        name="mlstm",
    )(z, z, z, kt, ifc, ifr, bias_r, bias_c, gain)


def _moba_body(qt_ref, k_ref, vt_ref, y_ref, kmean_s, qaug_s, acc_s, sc_s):
    blk = MOBA_BLOCK
    dh = MOBA_HEAD_DIM
    heads = qt_ref.shape[0] // dh
    n_blk = k_ref.shape[0] // blk
    nbp = kmean_s.shape[0]
    i = pl.program_id(1)

    @pl.when(i == 0)
    def _():
        kmean_s[...] = jnp.zeros_like(kmean_s)

        def mean_block(b, carry):
            kb = k_ref[pl.ds(pl.multiple_of(b * blk, blk), blk), :].astype(F32)
            kmean_s[pl.ds(b, 1), :] = jnp.mean(kb, axis=0, keepdims=True)
            return carry

        lax.fori_loop(0, n_blk, mean_block, 0)

    block_id = lax.broadcasted_iota(jnp.int32, (nbp, blk), 0).astype(F32)
    i_f = i.astype(F32)
    for hd in range(heads):
        hs = slice(hd * dh, (hd + 1) * dh)
        qt = qt_ref[hs, :]
        km = kmean_s[:, hs]
        km_hi = km.astype(BF16)
        km_lo = (km - km_hi.astype(F32)).astype(BF16)
        g = (jnp.dot(km_hi, qt, preferred_element_type=F32)
             + jnp.dot(km_lo, qt, preferred_element_type=F32))
        g = jnp.where(block_id < i_f, g, -jnp.inf)
        bias = jnp.full((nbp, blk), MASK_NEG, F32)
        for _ in range(MOBA_TOPK):
            mx = jnp.max(g, axis=0, keepdims=True)
            first = jnp.min(jnp.where(g == mx, block_id, float(nbp)), axis=0, keepdims=True)
            first = jnp.where(mx > -jnp.inf, first, -1.0)
            pick = block_id == first
            bias = jnp.where(pick, 0.0, bias)
            g = jnp.where(pick, -jnp.inf, g)
        if nbp < LANES:
            bias = jnp.concatenate([bias, jnp.zeros((LANES - nbp, blk), F32)], axis=0)
        qaug_s[hd] = jnp.concatenate([qt, bias.astype(BF16)], axis=0)
        acc_s[hd] = jnp.zeros((dh, blk), F32)

    lane = lax.broadcasted_iota(jnp.int32, (blk, LANES), 1)

    def block_onehot(j):
        return jnp.where(lane == jnp.where(j < i, j, -1), 1.0, 0.0).astype(BF16)

    def block_scores(hd, j, onehot):
        k_j = k_ref[pl.ds(pl.multiple_of(j * blk, blk), blk), hd * dh:(hd + 1) * dh]
        return jnp.dot(jnp.concatenate([k_j, onehot], axis=1), qaug_s[hd], preferred_element_type=F32)

    def attend(hd, stats, st, vt):
        m_old, l_old = stats
        m_new = jnp.maximum(m_old, jnp.max(st, axis=0, keepdims=True))
        alpha = jnp.exp2(m_old - m_new)
        p = jnp.exp2(st - m_new)
        l_new = alpha * l_old + jnp.sum(p, axis=0, keepdims=True)
        acc_s[hd] = alpha * acc_s[hd] + jnp.dot(vt, p.astype(BF16), preferred_element_type=F32)
        return m_new, l_new

    first = block_onehot(0)
    for hd in range(heads):
        sc_s[hd] = block_scores(hd, 0, first)

    def past_block(j, stats):
        onehot = block_onehot(j + 1)
        new_stats = []
        for hd in range(heads):
            st = sc_s[hd]
            sc_s[hd] = block_scores(hd, j + 1, onehot)
            new_stats.append(attend(hd, stats[hd], st, vt_ref[j, hd * dh:(hd + 1) * dh, :]))
        return tuple(new_stats)

    init = (jnp.full((1, blk), NEG_INIT, F32), jnp.zeros((1, blk), F32))
    stats = lax.fori_loop(0, i, past_block, (init,) * heads)

    key_pos = lax.broadcasted_iota(jnp.int32, (blk, blk), 0)
    query_pos = lax.broadcasted_iota(jnp.int32, (blk, blk), 1)
    causal = key_pos <= query_pos
    for hd in range(heads):
        hs = slice(hd * dh, (hd + 1) * dh)
        _, l = attend(hd, stats[hd], jnp.where(causal, sc_s[hd], -jnp.inf), vt_ref[i, hs, :])
        y_ref[:, hs] = (acc_s[hd] / l).T.astype(BF16)


def _moba(z, bqt, bvt):
    s = z.shape[0]
    blk = MOBA_BLOCK
    dh = MOBA_HEAD_DIM
    gw = MOBA_HEAD_GROUP * dh
    n_blk = s // blk
    nbp = -(-n_blk // SUBLANES) * SUBLANES
    k_tile = Z_BK * (MOBA_WIDTH // gw)
    return pl.pallas_call(
        _moba_body,
        grid=(MOBA_WIDTH // gw, n_blk),
        in_specs=[
            pl.BlockSpec((gw, blk), lambda g, i: (g, i)),
            _resident((s, gw), lambda g, i: (0, k_tile + g)),
            _resident((n_blk, gw, blk), lambda g, i: (0, g, 0)),
        ],
        out_specs=pl.BlockSpec((blk, gw), lambda g, i: (i, g)),
        out_shape=jax.ShapeDtypeStruct((s, MOBA_WIDTH), BF16),
        scratch_shapes=[
            pltpu.VMEM((nbp, gw), F32),
            pltpu.VMEM((MOBA_HEAD_GROUP, dh + LANES, blk), BF16),
            pltpu.VMEM((MOBA_HEAD_GROUP, dh, blk), F32),
            pltpu.VMEM((MOBA_HEAD_GROUP, blk, blk), F32),
        ],
        compiler_params=_params(48, "parallel", "arbitrary"),
        name="moba",
    )(bqt, z, bvt)


def _lru_body(x_ref, gate_ref, cw_ref, cb_ref, wa_ref, ba_ref, wx_ref, bx_ref, lam_ref,
              y_ref, tail_s, hlast_s, a_s, u_s):
    t, c = x_ref.shape

    @pl.when(pl.program_id(0) == 0)
    def _():
        tail_s[...] = jnp.zeros_like(tail_s)
        hlast_s[...] = jnp.zeros_like(hlast_s)

    x = x_ref[...].astype(F32)
    xx = jnp.concatenate([tail_s[...], x], axis=0)
    tail_s[...] = x[t - SUBLANES:, :]
    xc = cb_ref[...] + cw_ref[LRU_CONV - 1:LRU_CONV, :] * x
    for tap in range(LRU_CONV - 1):
        shift = LRU_CONV - 1 - tap
        xc = xc + cw_ref[tap:tap + 1, :] * xx[SUBLANES - shift:SUBLANES - shift + t, :]

    xcb = xc.astype(BF16)
    bd = LRU_BLOCK_DIM
    ra = jnp.concatenate(
        [jnp.dot(xcb[:, n * bd:(n + 1) * bd], wa_ref[n], preferred_element_type=F32) for n in range(LRU_BLOCKS)],
        axis=1)
    rx = jnp.concatenate(
        [jnp.dot(xcb[:, n * bd:(n + 1) * bd], wx_ref[n], preferred_element_type=F32) for n in range(LRU_BLOCKS)],
        axis=1)
    r = _sigmoid(ra + ba_ref[...])
    gi = _sigmoid(rx + bx_ref[...])
    log_a = -LRU_C * r * _softplus(-lam_ref[...])
    a = jnp.exp(log_a)
    a_s[...] = a
    u_s[...] = jnp.sqrt(-jnp.tanh(log_a) * (a * a + 1.0)) * (gi * xc)

    sub = lax.broadcasted_iota(jnp.int32, (SUBLANES, c), 0)

    def group(gidx, h_prev):
        r0 = pl.multiple_of(gidx * SUBLANES, SUBLANES)
        a = a_s[pl.ds(r0, SUBLANES), :]
        u = u_s[pl.ds(r0, SUBLANES), :]
        d = 1
        while d < SUBLANES:
            keep = sub >= d
            a_sh = jnp.where(keep, pltpu.roll(a, d, 0), 1.0)
            u_sh = jnp.where(keep, pltpu.roll(u, d, 0), 0.0)
            u = a * u_sh + u
            a = a * a_sh
            d *= 2
        h = u + a * h_prev
        u_s[pl.ds(r0, SUBLANES), :] = h
        return jnp.broadcast_to(h[SUBLANES - 1:SUBLANES, :], (SUBLANES, c))

    h_last = lax.fori_loop(0, t // SUBLANES, group, hlast_s[...], unroll=4)
    hlast_s[...] = h_last

    gt = gate_ref[...].astype(F32)
    gelu = 0.5 * gt * (1.0 + jnp.tanh(0.7978845608028654 * (gt + 0.044715 * (gt * gt * gt))))
    y_ref[...] = (u_s[...] * gelu).astype(BF16)


def _lru(z, conv_w, conv_b, w_a, b_a, w_x, b_x, lam):
    s = z.shape[0]
    c = LRU_WIDTH
    t = min(LRU_TILE, s)
    bd = LRU_BLOCK_DIM
    vec = pl.BlockSpec((1, c), lambda i: (0, 0))
    blockw = pl.BlockSpec((LRU_BLOCKS, bd, bd), lambda i: (0, 0, 0))
    return pl.pallas_call(
        _lru_body,
        grid=(s // t,),
        in_specs=[
            pl.BlockSpec((t, c), lambda i: (i, Z_CX)),
            pl.BlockSpec((t, c), lambda i: (i, Z_CG)),
            pl.BlockSpec((LRU_CONV, c), lambda i: (0, 0)),
            vec, blockw, vec, blockw, vec, vec,
        ],
        out_specs=pl.BlockSpec((t, c), lambda i: (i, 0)),
        out_shape=jax.ShapeDtypeStruct((s, c), BF16),
        scratch_shapes=[
            pltpu.VMEM((SUBLANES, c), F32),
            pltpu.VMEM((SUBLANES, c), F32),
            pltpu.VMEM((t, c), F32),
            pltpu.VMEM((t, c), F32),
        ],
        compiler_params=_params(32, "arbitrary"),
        name="rglru",
    )(z, z, conv_w, conv_b, w_a, b_a, w_x, b_x, lam)


def _merge_body(x_ref, ya_ref, yb_ref, yc_ref, ga_ref, gb_ref, gc_ref, wa_ref, wb_ref, wc_ref, wo_ref, o_ref):
    @pl.when(pl.program_id(1) == 0)
    def _():
        o_ref[...] = x_ref[...]

    merged = (_sigmoid(ga_ref[...].astype(F32)) * jnp.dot(ya_ref[...], wa_ref[...], preferred_element_type=F32)
              + _sigmoid(gb_ref[...].astype(F32)) * jnp.dot(yb_ref[...], wb_ref[...], preferred_element_type=F32)
              + _sigmoid(gc_ref[...].astype(F32)) * jnp.dot(yc_ref[...], wc_ref[...], preferred_element_type=F32))
    o_ref[...] += jnp.dot(merged.astype(BF16), wo_ref[...], preferred_element_type=F32)


def _merge(x, ya, yb, yc, z, wa, wb, wc, wo):
    s, d = x.shape
    tm = min(ROW_TILE, s)
    tn = MERGE_COL_TILE
    w = ya.shape[1]
    per_tile = MLSTM_WIDTH // tn
    ybranch = pl.BlockSpec((tm, w), lambda i, j: (i, 0))
    wup = pl.BlockSpec((w, tn), lambda i, j: (0, j))

    def gate_spec(base):
        return pl.BlockSpec((tm, tn), lambda i, j: (i, base * per_tile + j))

    return pl.pallas_call(
        _merge_body,
        grid=(s // tm, d // tn),
        in_specs=[
            pl.BlockSpec((tm, d), lambda i, j: (i, 0)),
            ybranch, ybranch, ybranch,
            gate_spec(Z_GA), gate_spec(Z_GB), gate_spec(Z_GC),
            wup, wup, wup,
            pl.BlockSpec((tn, d), lambda i, j: (j, 0)),
        ],
        out_specs=pl.BlockSpec((tm, d), lambda i, j: (i, 0)),
        out_shape=jax.ShapeDtypeStruct((s, d), F32),
        compiler_params=_params(48, "parallel", "arbitrary"),
        name="merge",
    )(x, ya, yb, yc, z, z, z, wa, wb, wc, wo)


def _row(v):
    return v.reshape(1, -1).astype(F32)


def _proj_weights(w_in):
    w = MLSTM_WIDTH
    n_gate = 2 * MLSTM_HEADS
    b0 = 4 * w + n_gate
    gates = w_in[:, 4 * w:b0]
    w_main = jnp.concatenate(
        [w_in[:, 0:w], w_in[:, 2 * w:4 * w], w_in[:, b0 + w:b0 + 2 * w], w_in[:, b0 + 3 * w:]], axis=1).astype(BF16)
    w_t = jnp.concatenate([w_in[:, w:2 * w], w_in[:, b0:b0 + w], w_in[:, b0 + 2 * w:b0 + 3 * w]], axis=1).astype(BF16)
    w_if = jnp.pad(gates, ((0, 0), (0, GATE_PAD - n_gate))).astype(BF16)
    w_ift = jnp.pad(gates.T, ((0, GATE_ROWS - n_gate), (0, 0))).astype(BF16)
    return w_main, w_t, w_if, w_ift


def kernel(x, ffn1_norm, ffn1_w_gate, ffn1_w_up, ffn1_w_down, mix_norm, w_in, mlstm_b_i, mlstm_b_f, mlstm_head_norm, lru_conv_w, lru_conv_b, lru_w_a, lru_b_a, lru_w_x, lru_b_x, lru_lambda, w_up_a, w_up_b, w_up_c, w_out, ffn2_norm, ffn2_w_gate, ffn2_w_up, ffn2_w_down, final_norm):
    b, s, d = x.shape
    depth = w_in.shape[0]
    outs = []
    for bi in range(b):
        xs = x[bi]
        for l in range(depth):
            xs = _ffn(xs, _row(ffn1_norm[l]), ffn1_w_gate, ffn1_w_up, ffn1_w_down, l)
            z, akt, bqt, bvt, ifc, ifr = _proj(xs, _row(mix_norm[l]), *_proj_weights(w_in[l]))
            gate_bias = jnp.concatenate([mlstm_b_i[l], mlstm_b_f[l]]).astype(F32)
            bias_r = jnp.pad(gate_bias, (0, GATE_PAD - gate_bias.shape[0])).reshape(1, GATE_PAD)
            bias_c = jnp.pad(gate_bias, (0, GATE_ROWS - gate_bias.shape[0])).reshape(GATE_ROWS, 1)
            ya = _mlstm(z, akt, ifc, ifr, bias_r, bias_c, _row(mlstm_head_norm[l]))
            yb = _moba(z, bqt, bvt)
            yc = _lru(z, lru_conv_w[l].astype(F32), _row(lru_conv_b[l]), lru_w_a[l].astype(BF16), _row(lru_b_a[l]),
                      lru_w_x[l].astype(BF16), _row(lru_b_x[l]), _row(lru_lambda[l]))
            xs = _merge(xs, ya, yb, yc, z, w_up_a[l].astype(BF16), w_up_b[l].astype(BF16),
                        w_up_c[l].astype(BF16), w_out[l].astype(BF16))
            xs = _ffn(xs, _row(ffn2_norm[l]), ffn2_w_gate, ffn2_w_up, ffn2_w_down, l,
                      final_g=_row(final_norm) if l == depth - 1 else None)
        outs.append(xs)
    return jnp.stack(outs).astype(x.dtype)
```

```python
import functools
import math

import jax
import jax.numpy as jnp
from jax import lax
from jax.experimental import pallas as pl
from jax.experimental.pallas import tpu as pltpu

F32 = jnp.float32
BF16 = jnp.bfloat16

D_MODEL = 2048
EPS = 1e-6
MLSTM_HEADS = 4
MLSTM_WIDTH = D_MODEL // 2
MLSTM_HEAD_DIM = MLSTM_WIDTH // MLSTM_HEADS
MOBA_HEADS = 8
MOBA_WIDTH = D_MODEL // 2
MOBA_HEAD_DIM = MOBA_WIDTH // MOBA_HEADS
MOBA_BLOCK = 256
MOBA_TOPK = 3
LRU_WIDTH = D_MODEL // 2
LRU_BLOCKS = 8
LRU_BLOCK_DIM = LRU_WIDTH // LRU_BLOCKS
LRU_CONV = 4
LRU_C = 8.0
D_FF = ((8 * D_MODEL // 3 + 127) // 128) * 128

LANES = 128
SUBLANES = 8
MIB = 1024 * 1024

ROW_TILE = 512
FFN_ROW_TILE = 1024
FF_TILE = 256
PROJ_COL_TILE = 2048
MERGE_COL_TILE = 512
MLSTM_CHUNK = 256
LRU_TILE = 512
MOBA_HEAD_GROUP = 8
GATE_PAD = LANES
GATE_ROWS = 16
MASK_NEG = -1e30
NEG_INIT = -1e30
MOBA_Q_SCALE = MOBA_HEAD_DIM ** -0.5 * math.log2(math.e)

BF16_SUBLANES = 16
MOBA_VROWS = MOBA_HEAD_DIM + BF16_SUBLANES

Z_AQ, Z_AV, Z_AO, Z_BK, Z_CX, Z_CG, Z_GA, Z_GB, Z_GC = 0, 1, 2, 3, 4, 5, 6, 8, 10
Z_COLS = 12 * 1024
T_COLS = 3 * 1024
PREP_ROWS = 128

NT_DIMS = (((1,), (1,)), ((), ()))


def _params(vmem_mib, *semantics):
    return pltpu.CompilerParams(dimension_semantics=semantics, vmem_limit_bytes=vmem_mib * MIB)


def _resident(block_shape, index_map):
    return pl.BlockSpec(block_shape, index_map, pipeline_mode=pl.Buffered(1))


def _rms_norm(x, g):
    return x * lax.rsqrt(jnp.mean(x * x, axis=-1, keepdims=True) + EPS) * g


def _sigmoid(x):
    return 1.0 / (1.0 + jnp.exp(-x))


def _sigmoid_tanh(x):
    return 0.5 * (jnp.tanh(0.5 * x) + 1.0)


def _log_sigmoid(x):
    return jnp.minimum(x, 0.0) - jnp.log1p(jnp.exp(-jnp.abs(x)))


def _softplus(x):
    return jnp.maximum(x, 0.0) + jnp.log1p(jnp.exp(-jnp.abs(x)))


def _split3(x):
    hi = x.astype(BF16)
    r1 = x - hi.astype(F32)
    mid = r1.astype(BF16)
    lo = (r1 - mid.astype(F32)).astype(BF16)
    return hi, mid, lo


def _ffn_body(x_ref, g_ref, wg_ref, wu_ref, wd_ref, *rest, final):
    if final:
        fg_ref, o_ref, h_s = rest
    else:
        o_ref, h_s = rest
    j = pl.program_id(1)
    tf = wg_ref.shape[1]

    @pl.when(j == 0)
    def _():
        x = x_ref[...]
        h_s[...] = _rms_norm(x, g_ref[...]).astype(BF16)
        o_ref[...] = x

    h = h_s[...]
    valid = D_FF - j * tf
    a = jnp.dot(h, wg_ref[...].astype(BF16), preferred_element_type=F32)
    b = jnp.dot(h, wu_ref[...].astype(BF16), preferred_element_type=F32)
    col = lax.broadcasted_iota(jnp.int32, (1, tf), 1)
    act = jnp.where(col < valid, 0.5 * (a * _sigmoid(a) * b), 0.0).astype(BF16)
    row = lax.broadcasted_iota(jnp.int32, (tf, 1), 0)
    wd = jnp.where(row < valid, wd_ref[...], 0.0).astype(BF16)
    o_ref[...] += jnp.dot(act, wd, preferred_element_type=F32)

    if final:
        @pl.when(j == pl.num_programs(1) - 1)
        def _():
            o_ref[...] = _rms_norm(o_ref[...], fg_ref[...])


def _ffn(x, norm_g, wg, wu, wd, layer, final_g=None):
    s, d = x.shape
    tm = min(FFN_ROW_TILE, s)
    tf = FF_TILE
    final = final_g is not None
    in_specs = [
        _resident((tm, d), lambda i, j: (i, 0)),
        pl.BlockSpec((1, d), lambda i, j: (0, 0)),
        pl.BlockSpec((None, d, tf), lambda i, j: (layer, 0, j)),
        pl.BlockSpec((None, d, tf), lambda i, j: (layer, 0, j)),
        pl.BlockSpec((None, tf, d), lambda i, j: (layer, j, 0)),
    ]
    args = [x, norm_g, wg, wu, wd]
    if final:
        in_specs.append(pl.BlockSpec((1, d), lambda i, j: (0, 0)))
        args.append(final_g)
    return pl.pallas_call(
        functools.partial(_ffn_body, final=final),
        grid=(s // tm, pl.cdiv(D_FF, tf)),
        in_specs=in_specs,
        out_specs=pl.BlockSpec((tm, d), lambda i, j: (i, 0)),
        out_shape=jax.ShapeDtypeStruct((s, d), F32),
        scratch_shapes=[pltpu.VMEM((tm, d), BF16)],
        compiler_params=_params(56, "parallel", "arbitrary"),
        name="ffn",
    )(*args)


def _prep_body(w_ref, main_ref, t_ref):
    w = MLSTM_WIDTH
    b0 = 4 * w + 2 * MLSTM_HEADS
    main_ref[:, 0:w] = w_ref[:, 0:w].astype(BF16)
    main_ref[:, w:3 * w] = w_ref[:, 2 * w:4 * w].astype(BF16)
    main_ref[:, 3 * w:4 * w] = w_ref[:, b0 + w:b0 + 2 * w].astype(BF16)
    main_ref[:, 4 * w:] = w_ref[:, b0 + 3 * w:].astype(BF16)
    t_ref[:, 0:w] = w_ref[:, w:2 * w].astype(BF16)
    t_ref[:, w:2 * w] = w_ref[:, b0:b0 + w].astype(BF16)
    t_ref[:, 2 * w:] = w_ref[:, b0 + 2 * w:b0 + 3 * w].astype(BF16)


def _prep_w_in(w_in):
    depth, d, n_in = w_in.shape
    rk = PREP_ROWS
    return pl.pallas_call(
        _prep_body,
        grid=(depth, d // rk),
        in_specs=[pl.BlockSpec((None, rk, n_in), lambda l, r: (l, r, 0))],
        out_specs=[
            pl.BlockSpec((None, rk, Z_COLS), lambda l, r: (l, r, 0)),
            pl.BlockSpec((None, rk, T_COLS), lambda l, r: (l, r, 0)),
        ],
        out_shape=[
            jax.ShapeDtypeStruct((depth, d, Z_COLS), BF16),
            jax.ShapeDtypeStruct((depth, d, T_COLS), BF16),
        ],
        compiler_params=_params(48, "parallel", "parallel"),
        name="prep_w_in",
    )(w_in)


def _proj_body(x_ref, g_ref, w_ref, wt_ref, wif_ref, wift_ref,
               z_ref, akt_ref, bqt_ref, bvt_ref, ifc_ref, ifr_ref, h_s):
    @pl.when(pl.program_id(1) == 0)
    def _():
        h_s[...] = _rms_norm(x_ref[...], g_ref[...]).astype(BF16)
        h = h_s[...]
        w = MLSTM_WIDTH
        tm = h.shape[0]
        akt_ref[...] = jnp.dot(h, wt_ref[:, 0:w], preferred_element_type=F32).astype(BF16).T
        bq = jnp.dot(h, wt_ref[:, w:2 * w], preferred_element_type=F32) * MOBA_Q_SCALE
        bqt_ref[...] = bq.astype(BF16).T
        bvt = jnp.dot(h, wt_ref[:, 2 * w:3 * w], preferred_element_type=F32).astype(BF16).T
        ones = jnp.ones((BF16_SUBLANES, tm), BF16)
        dh = MOBA_HEAD_DIM
        bvt = jnp.concatenate(
            [part for hd in range(MOBA_HEADS) for part in (bvt[hd * dh:(hd + 1) * dh, :], ones)], axis=0)
        for n in range(bvt_ref.shape[0]):
            bvt_ref[n] = bvt[:, n * MOBA_BLOCK:(n + 1) * MOBA_BLOCK]
        ifc_ref[...] = jnp.dot(h, wif_ref[...], preferred_element_type=F32)
        ifr_ref[...] = lax.dot_general(wift_ref[...], h, NT_DIMS, preferred_element_type=F32)

    z_ref[...] = jnp.dot(h_s[...], w_ref[...], preferred_element_type=F32).astype(BF16)


def _proj(x, norm_g, w_main, w_t, w_if, w_ift, layer):
    s, d = x.shape
    tm = min(ROW_TILE, s)
    tn = PROJ_COL_TILE
    blk = MOBA_BLOCK
    vrows = MOBA_HEADS * MOBA_VROWS
    return pl.pallas_call(
        _proj_body,
        grid=(s // tm, Z_COLS // tn),
        in_specs=[
            _resident((tm, d), lambda i, j: (i, 0)),
            pl.BlockSpec((1, d), lambda i, j: (0, 0)),
            pl.BlockSpec((None, d, tn), lambda i, j: (layer, 0, j)),
            _resident((None, d, T_COLS), lambda i, j: (layer, 0, 0)),
            pl.BlockSpec((d, GATE_PAD), lambda i, j: (0, 0)),
            pl.BlockSpec((GATE_ROWS, d), lambda i, j: (0, 0)),
        ],
        out_specs=[
            pl.BlockSpec((tm, tn), lambda i, j: (i, j)),
            pl.BlockSpec((MLSTM_WIDTH, tm), lambda i, j: (0, i)),
            pl.BlockSpec((MOBA_WIDTH, tm), lambda i, j: (0, i)),
            pl.BlockSpec((tm // blk, vrows, blk), lambda i, j: (i, 0, 0)),
            pl.BlockSpec((tm, GATE_PAD), lambda i, j: (i, 0)),
            pl.BlockSpec((GATE_ROWS, tm), lambda i, j: (0, i)),
        ],
        out_shape=[
            jax.ShapeDtypeStruct((s, Z_COLS), BF16),
            jax.ShapeDtypeStruct((MLSTM_WIDTH, s), BF16),
            jax.ShapeDtypeStruct((MOBA_WIDTH, s), BF16),
            jax.ShapeDtypeStruct((s // blk, vrows, blk), BF16),
            jax.ShapeDtypeStruct((s, GATE_PAD), F32),
            jax.ShapeDtypeStruct((GATE_ROWS, s), F32),
        ],
        scratch_shapes=[pltpu.VMEM((tm, d), BF16)],
        compiler_params=_params(56, "parallel", "arbitrary"),
        name="proj",
    )(x, norm_g, w_main, w_t, w_if, w_ift)


def _mlstm_body(q_ref, v_ref, o_ref, kt_ref, ifc_ref, ifr_ref, bias_r_ref, bias_c_ref, gain_ref,
                y_ref, ct_s, m_s):
    chunk = q_ref.shape[0]
    dh = MLSTM_HEAD_DIM

    @pl.when(pl.program_id(0) == 0)
    def _():
        ct_s[...] = jnp.zeros_like(ct_s)
        m_s[...] = jnp.zeros_like(m_s)

    gates_c = ifc_ref[...] + bias_r_ref[...]
    gates_r = ifr_ref[...] + bias_c_ref[...]
    row = lax.broadcasted_iota(jnp.int32, (chunk, chunk), 0)
    col = lax.broadcasted_iota(jnp.int32, (chunk, chunk), 1)
    causal = row >= col
    tril = jnp.where(causal, 1.0, 0.0).astype(BF16)
    triu = jnp.where(row <= col, 1.0, 0.0).astype(BF16)
    bcum_c = sum(jnp.dot(tril, p, preferred_element_type=F32) for p in _split3(_log_sigmoid(gates_c)))
    bcum_r = sum(jnp.dot(p, triu, preferred_element_type=F32) for p in _split3(_log_sigmoid(gates_r)))
    ones_col = jnp.where(lax.broadcasted_iota(jnp.int32, (chunk, LANES), 1) == 0, 1.0, 0.0).astype(BF16)
    k_scale = dh ** -0.5

    for hd in range(MLSTM_HEADS):
        sl = slice(hd * dh, (hd + 1) * dh)
        fcol = MLSTM_HEADS + hd
        q = q_ref[:, sl]
        kt = kt_ref[sl, :]
        v_aug = jnp.concatenate([v_ref[:, sl], ones_col], axis=1)
        bc = bcum_c[:, fcol:fcol + 1]
        br = bcum_r[fcol:fcol + 1, :]
        i_r = gates_r[hd:hd + 1, :]
        m_prev = m_s[hd:hd + 1, 0:1]
        ct = ct_s[hd]

        d_intra = jnp.where(causal, bc - br + i_r, -jnp.inf)
        d_inter = bc + m_prev
        m_t = jnp.maximum(jnp.max(d_intra, axis=1, keepdims=True), d_inter)
        w_intra = jnp.exp(d_intra - m_t)
        w_inter = jnp.exp(d_inter - m_t)
        qk = jnp.dot(q, kt, preferred_element_type=F32)
        sc = (qk * k_scale * w_intra).astype(BF16)
        numden = (jnp.dot(sc, v_aug, preferred_element_type=F32)
                  + w_inter * jnp.dot(q, ct.astype(BF16), preferred_element_type=F32))
        num = numden[:, :dh]
        den = numden[:, dh:dh + 1]
        h_out = num / jnp.maximum(jnp.abs(den), jnp.exp(-m_t))
        h_out = h_out * lax.rsqrt(jnp.mean(h_out * h_out, axis=-1, keepdims=True) + EPS)
        h_out = h_out * gain_ref[:, sl]
        y_ref[:, sl] = (_sigmoid(o_ref[:, sl].astype(F32)) * h_out).astype(BF16)

        b_last = bc[chunk - 1:chunk, :]
        g_r = b_last - br + i_r
        m_new = jnp.maximum(b_last + m_prev, jnp.max(g_r, axis=1, keepdims=True))
        decay = jnp.exp(b_last + m_prev - m_new)
        w_r = jnp.exp(g_r - m_new) * k_scale
        ktw = (kt.astype(F32) * w_r).astype(BF16)
        ct_s[hd] = decay * ct + jnp.dot(ktw, v_aug, preferred_element_type=F32)
        m_s[hd:hd + 1, :] = jnp.broadcast_to(m_new, (1, LANES))


def _mlstm(z, kt, ifc, ifr, bias_r, bias_c, gain):
    s = z.shape[0]
    chunk = min(MLSTM_CHUNK, s)
    w = MLSTM_WIDTH
    return pl.pallas_call(
        _mlstm_body,
        grid=(s // chunk,),
        in_specs=[
            pl.BlockSpec((chunk, w), lambda c: (c, Z_AQ)),
            pl.BlockSpec((chunk, w), lambda c: (c, Z_AV)),
            pl.BlockSpec((chunk, w), lambda c: (c, Z_AO)),
            pl.BlockSpec((w, chunk), lambda c: (0, c)),
            pl.BlockSpec((chunk, GATE_PAD), lambda c: (c, 0)),
            pl.BlockSpec((GATE_ROWS, chunk), lambda c: (0, c)),
            pl.BlockSpec((1, GATE_PAD), lambda c: (0, 0)),
            pl.BlockSpec((GATE_ROWS, 1), lambda c: (0, 0)),
            pl.BlockSpec((1, w), lambda c: (0, 0)),
        ],
        out_specs=pl.BlockSpec((chunk, w), lambda c: (c, 0)),
        out_shape=jax.ShapeDtypeStruct((s, w), BF16),
        scratch_shapes=[
            pltpu.VMEM((MLSTM_HEADS, MLSTM_HEAD_DIM, MLSTM_HEAD_DIM + LANES), F32),
            pltpu.VMEM((SUBLANES, LANES), F32),
        ],
        compiler_params=_params(32, "arbitrary"),
        name="mlstm",
    )(z, z, z, kt, ifc, ifr, bias_r, bias_c, gain)


def _moba_body(qt_ref, k_ref, vt_ref, y_ref, kmean_s, qaug_s, acc_s, sc_s):
    blk = MOBA_BLOCK
    dh = MOBA_HEAD_DIM
    vr = MOBA_VROWS
    heads = qt_ref.shape[0] // dh
    n_blk = k_ref.shape[0] // blk
    nbp = kmean_s.shape[0]
    i = pl.program_id(1)

    @pl.when(i == 0)
    def _():
        kmean_s[...] = jnp.zeros_like(kmean_s)

        def mean_block(b, carry):
            kb = k_ref[pl.ds(pl.multiple_of(b * blk, blk), blk), :].astype(F32)
            kmean_s[pl.ds(b, 1), :] = jnp.mean(kb, axis=0, keepdims=True)
            return carry

        lax.fori_loop(0, n_blk, mean_block, 0)

    block_id = lax.broadcasted_iota(jnp.int32, (nbp, blk), 0).astype(F32)
    i_f = i.astype(F32)
    for hd in range(heads):
        hs = slice(hd * dh, (hd + 1) * dh)
        qt = qt_ref[hs, :]
        km = kmean_s[:, hs]
        km_hi = km.astype(BF16)
        km_lo = (km - km_hi.astype(F32)).astype(BF16)
        g = (jnp.dot(km_hi, qt, preferred_element_type=F32)
             + jnp.dot(km_lo, qt, preferred_element_type=F32))
        g = jnp.where(block_id < i_f, g, -jnp.inf)
        bias = jnp.full((nbp, blk), MASK_NEG, F32)
        for _ in range(MOBA_TOPK):
            mx = jnp.max(g, axis=0, keepdims=True)
            first = jnp.min(jnp.where(g == mx, block_id, float(nbp)), axis=0, keepdims=True)
            first = jnp.where(mx > -jnp.inf, first, -1.0)
            pick = block_id == first
            bias = jnp.where(pick, 0.0, bias)
            g = jnp.where(pick, -jnp.inf, g)
        if nbp < LANES:
            bias = jnp.concatenate([bias, jnp.zeros((LANES - nbp, blk), F32)], axis=0)
        qaug_s[hd] = jnp.concatenate([qt, bias.astype(BF16)], axis=0)
        acc_s[hd] = jnp.zeros((vr, blk), F32)

    lane = lax.broadcasted_iota(jnp.int32, (blk, LANES), 1)

    def block_onehot(j):
        return jnp.where(lane == jnp.where(j < i, j, -1), 1.0, 0.0).astype(BF16)

    def block_scores(hd, j, onehot):
        k_j = k_ref[pl.ds(pl.multiple_of(j * blk, blk), blk), hd * dh:(hd + 1) * dh]
        return jnp.dot(jnp.concatenate([k_j, onehot], axis=1), qaug_s[hd], preferred_element_type=F32)

    def attend(hd, m_old, st, j):
        m_new = jnp.maximum(m_old, jnp.max(st, axis=0, keepdims=True))
        alpha = jnp.exp2(m_old - m_new)
        p = jnp.exp2(st - m_new).astype(BF16)
        vt = vt_ref[j, hd * vr:(hd + 1) * vr, :]
        acc_s[hd] = alpha * acc_s[hd] + jnp.dot(vt, p, preferred_element_type=F32)
        return m_new

    first = block_onehot(0)
    for hd in range(heads):
        sc_s[hd] = block_scores(hd, 0, first)

    def past_block(j, maxima):
        onehot = block_onehot(j + 1)
        new_maxima = []
        for hd in range(heads):
            st = sc_s[hd]
            sc_s[hd] = block_scores(hd, j + 1, onehot)
            new_maxima.append(attend(hd, maxima[hd], st, j))
        return tuple(new_maxima)

    maxima = lax.fori_loop(0, i, past_block, (jnp.full((1, blk), NEG_INIT, F32),) * heads)

    key_pos = lax.broadcasted_iota(jnp.int32, (blk, blk), 0)
    query_pos = lax.broadcasted_iota(jnp.int32, (blk, blk), 1)
    causal = key_pos <= query_pos
    for hd in range(heads):
        attend(hd, maxima[hd], jnp.where(causal, sc_s[hd], -jnp.inf), i)
        out = acc_s[hd, 0:dh, :] / acc_s[hd, dh:dh + 1, :]
        y_ref[:, hd * dh:(hd + 1) * dh] = out.T.astype(BF16)


def _moba(z, bqt, bvt):
    s = z.shape[0]
    blk = MOBA_BLOCK
    dh = MOBA_HEAD_DIM
    gw = MOBA_HEAD_GROUP * dh
    n_blk = s // blk
    nbp = -(-n_blk // SUBLANES) * SUBLANES
    k_tile = Z_BK * (MOBA_WIDTH // gw)
    return pl.pallas_call(
        _moba_body,
        grid=(MOBA_WIDTH // gw, n_blk),
        in_specs=[
            pl.BlockSpec((gw, blk), lambda g, i: (g, i)),
            _resident((s, gw), lambda g, i: (0, k_tile + g)),
            _resident((n_blk, MOBA_HEAD_GROUP * MOBA_VROWS, blk), lambda g, i: (0, g, 0)),
        ],
        out_specs=pl.BlockSpec((blk, gw), lambda g, i: (i, g)),
        out_shape=jax.ShapeDtypeStruct((s, MOBA_WIDTH), BF16),
        scratch_shapes=[
            pltpu.VMEM((nbp, gw), F32),
            pltpu.VMEM((MOBA_HEAD_GROUP, dh + LANES, blk), BF16),
            pltpu.VMEM((MOBA_HEAD_GROUP, MOBA_VROWS, blk), F32),
            pltpu.VMEM((MOBA_HEAD_GROUP, blk, blk), F32),
        ],
        compiler_params=_params(48, "parallel", "arbitrary"),
        name="moba",
    )(bqt, z, bvt)


def _lru_body(x_ref, gate_ref, cw_ref, cb_ref, wa_ref, ba_ref, wx_ref, bx_ref, lam_ref,
              y_ref, tail_s, hlast_s, a_s, u_s):
    t, c = x_ref.shape

    @pl.when(pl.program_id(0) == 0)
    def _():
        tail_s[...] = jnp.zeros_like(tail_s)
        hlast_s[...] = jnp.zeros_like(hlast_s)

    x = x_ref[...].astype(F32)
    xx = jnp.concatenate([tail_s[...], x], axis=0)
    tail_s[...] = x[t - SUBLANES:, :]
    xc = cb_ref[...] + cw_ref[LRU_CONV - 1:LRU_CONV, :] * x
    for tap in range(LRU_CONV - 1):
        shift = LRU_CONV - 1 - tap
        xc = xc + cw_ref[tap:tap + 1, :] * xx[SUBLANES - shift:SUBLANES - shift + t, :]

    xcb = xc.astype(BF16)
    bd = LRU_BLOCK_DIM
    ra = jnp.concatenate(
        [jnp.dot(xcb[:, n * bd:(n + 1) * bd], wa_ref[n], preferred_element_type=F32) for n in range(LRU_BLOCKS)],
        axis=1)
    rx = jnp.concatenate(
        [jnp.dot(xcb[:, n * bd:(n + 1) * bd], wx_ref[n], preferred_element_type=F32) for n in range(LRU_BLOCKS)],
        axis=1)
    r = _sigmoid_tanh(ra + ba_ref[...])
    gi = _sigmoid_tanh(rx + bx_ref[...])
    log_a = -LRU_C * r * _softplus(-lam_ref[...])
    a = jnp.exp(log_a)
    a_s[...] = a
    s = -jnp.tanh(log_a) * (a * a + 1.0)
    root = jnp.where(s > 0.0, s * lax.rsqrt(s), 0.0)
    u_s[...] = root * (gi * xc)

    sub = lax.broadcasted_iota(jnp.int32, (SUBLANES, c), 0)

    def group(gidx, h_prev):
        r0 = pl.multiple_of(gidx * SUBLANES, SUBLANES)
        a = a_s[pl.ds(r0, SUBLANES), :]
        u = u_s[pl.ds(r0, SUBLANES), :]
        d = 1
        while d < SUBLANES:
            keep = sub >= d
            a_sh = jnp.where(keep, pltpu.roll(a, d, 0), 1.0)
            u_sh = jnp.where(keep, pltpu.roll(u, d, 0), 0.0)
            u = a * u_sh + u
            a = a * a_sh
            d *= 2
        h = u + a * h_prev
        u_s[pl.ds(r0, SUBLANES), :] = h
        return jnp.broadcast_to(h[SUBLANES - 1:SUBLANES, :], (SUBLANES, c))

    h_last = lax.fori_loop(0, t // SUBLANES, group, hlast_s[...], unroll=4)
    hlast_s[...] = h_last

    gt = gate_ref[...].astype(F32)
    gelu = 0.5 * gt * (1.0 + jnp.tanh(0.7978845608028654 * (gt + 0.044715 * (gt * gt * gt))))
    y_ref[...] = (u_s[...] * gelu).astype(BF16)


def _lru(z, conv_w, conv_b, w_a, b_a, w_x, b_x, lam):
    s = z.shape[0]
    c = LRU_WIDTH
    t = min(LRU_TILE, s)
    bd = LRU_BLOCK_DIM
    vec = pl.BlockSpec((1, c), lambda i: (0, 0))
    blockw = pl.BlockSpec((LRU_BLOCKS, bd, bd), lambda i: (0, 0, 0))
    return pl.pallas_call(
        _lru_body,
        grid=(s // t,),
        in_specs=[
            pl.BlockSpec((t, c), lambda i: (i, Z_CX)),
            pl.BlockSpec((t, c), lambda i: (i, Z_CG)),
            pl.BlockSpec((LRU_CONV, c), lambda i: (0, 0)),
            vec, blockw, vec, blockw, vec, vec,
        ],
        out_specs=pl.BlockSpec((t, c), lambda i: (i, 0)),
        out_shape=jax.ShapeDtypeStruct((s, c), BF16),
        scratch_shapes=[
            pltpu.VMEM((SUBLANES, c), F32),
            pltpu.VMEM((SUBLANES, c), F32),
            pltpu.VMEM((t, c), F32),
            pltpu.VMEM((t, c), F32),
        ],
        compiler_params=_params(32, "arbitrary"),
        name="rglru",
    )(z, z, conv_w, conv_b, w_a, b_a, w_x, b_x, lam)


def _merge_body(x_ref, ya_ref, yb_ref, yc_ref, ga_ref, gb_ref, gc_ref, wa_ref, wb_ref, wc_ref, wo_ref, o_ref):
    @pl.when(pl.program_id(1) == 0)
    def _():
        o_ref[...] = x_ref[...]

    merged = (_sigmoid(ga_ref[...].astype(F32)) * jnp.dot(ya_ref[...], wa_ref[...], preferred_element_type=F32)
              + _sigmoid(gb_ref[...].astype(F32)) * jnp.dot(yb_ref[...], wb_ref[...], preferred_element_type=F32)
              + _sigmoid(gc_ref[...].astype(F32)) * jnp.dot(yc_ref[...], wc_ref[...], preferred_element_type=F32))
    o_ref[...] += jnp.dot(merged.astype(BF16), wo_ref[...], preferred_element_type=F32)


def _merge(x, ya, yb, yc, z, wa, wb, wc, wo):
    s, d = x.shape
    tm = min(ROW_TILE, s)
    tn = MERGE_COL_TILE
    w = ya.shape[1]
    per_tile = MLSTM_WIDTH // tn
    ybranch = pl.BlockSpec((tm, w), lambda i, j: (i, 0))
    wup = pl.BlockSpec((w, tn), lambda i, j: (0, j))

    def gate_spec(base):
        return pl.BlockSpec((tm, tn), lambda i, j: (i, base * per_tile + j))

    return pl.pallas_call(
        _merge_body,
        grid=(s // tm, d // tn),
        in_specs=[
            pl.BlockSpec((tm, d), lambda i, j: (i, 0)),
            ybranch, ybranch, ybranch,
            gate_spec(Z_GA), gate_spec(Z_GB), gate_spec(Z_GC),
            wup, wup, wup,
            pl.BlockSpec((tn, d), lambda i, j: (j, 0)),
        ],
        out_specs=pl.BlockSpec((tm, d), lambda i, j: (i, 0)),
        out_shape=jax.ShapeDtypeStruct((s, d), F32),
        compiler_params=_params(48, "parallel", "arbitrary"),
        name="merge",
    )(x, ya, yb, yc, z, z, z, wa, wb, wc, wo)


def _row(v):
    return v.reshape(1, -1).astype(F32)


def _gate_weights(w_in_l):
    n_gate = 2 * MLSTM_HEADS
    gates = w_in_l[:, 4 * MLSTM_WIDTH:4 * MLSTM_WIDTH + n_gate]
    w_if = jnp.pad(gates, ((0, 0), (0, GATE_PAD - n_gate))).astype(BF16)
    w_ift = jnp.pad(gates.T, ((0, GATE_ROWS - n_gate), (0, 0))).astype(BF16)
    return w_if, w_ift


def kernel(x, ffn1_norm, ffn1_w_gate, ffn1_w_up, ffn1_w_down, mix_norm, w_in, mlstm_b_i, mlstm_b_f, mlstm_head_norm, lru_conv_w, lru_conv_b, lru_w_a, lru_b_a, lru_w_x, lru_b_x, lru_lambda, w_up_a, w_up_b, w_up_c, w_out, ffn2_norm, ffn2_w_gate, ffn2_w_up, ffn2_w_down, final_norm):
    b, s, d = x.shape
    depth = w_in.shape[0]
    w_main, w_t = _prep_w_in(w_in)
    outs = []
    for bi in range(b):
        xs = x[bi]
        for l in range(depth):
            xs = _ffn(xs, _row(ffn1_norm[l]), ffn1_w_gate, ffn1_w_up, ffn1_w_down, l)
            z, akt, bqt, bvt, ifc, ifr = _proj(xs, _row(mix_norm[l]), w_main, w_t, *_gate_weights(w_in[l]), l)
            gate_bias = jnp.concatenate([mlstm_b_i[l], mlstm_b_f[l]]).astype(F32)
            bias_r = jnp.pad(gate_bias, (0, GATE_PAD - gate_bias.shape[0])).reshape(1, GATE_PAD)
            bias_c = jnp.pad(gate_bias, (0, GATE_ROWS - gate_bias.shape[0])).reshape(GATE_ROWS, 1)
            ya = _mlstm(z, akt, ifc, ifr, bias_r, bias_c, _row(mlstm_head_norm[l]))
            yb = _moba(z, bqt, bvt)
            yc = _lru(z, lru_conv_w[l].astype(F32), _row(lru_conv_b[l]), lru_w_a[l].astype(BF16), _row(lru_b_a[l]),
                      lru_w_x[l].astype(BF16), _row(lru_b_x[l]), _row(lru_lambda[l]))
            xs = _merge(xs, ya, yb, yc, z, w_up_a[l].astype(BF16), w_up_b[l].astype(BF16),
                        w_up_c[l].astype(BF16), w_out[l].astype(BF16))
            xs = _ffn(xs, _row(ffn2_norm[l]), ffn2_w_gate, ffn2_w_up, ffn2_w_down, l,
                      final_g=_row(final_norm) if l == depth - 1 else None)
        outs.append(xs)
    return jnp.stack(outs).astype(x.dtype)
```

```python
import functools
import math

import jax
import jax.numpy as jnp
from jax import lax
from jax.experimental import pallas as pl
from jax.experimental.pallas import tpu as pltpu

F32 = jnp.float32
BF16 = jnp.bfloat16

D_MODEL = 2048
EPS = 1e-6
MLSTM_HEADS = 4
MLSTM_WIDTH = D_MODEL // 2
MLSTM_HEAD_DIM = MLSTM_WIDTH // MLSTM_HEADS
MOBA_HEADS = 8
MOBA_WIDTH = D_MODEL // 2
MOBA_HEAD_DIM = MOBA_WIDTH // MOBA_HEADS
MOBA_BLOCK = 256
MOBA_TOPK = 3
LRU_WIDTH = D_MODEL // 2
LRU_BLOCKS = 8
LRU_BLOCK_DIM = LRU_WIDTH // LRU_BLOCKS
LRU_CONV = 4
LRU_C = 8.0
D_FF = ((8 * D_MODEL // 3 + 127) // 128) * 128

LANES = 128
SUBLANES = 8
MIB = 1024 * 1024

ROW_TILE = 512
FFN_ROW_TILE = 1024
FF_TILE = 256
PROJ_COL_TILE = 2048
MERGE_COL_TILE = 512
MLSTM_CHUNK = 256
LRU_TILE = 512
MOBA_HEAD_GROUP = 8
GATE_PAD = LANES
GATE_ROWS = 16
MASK_NEG = -1e30
NEG_INIT = -1e30
MOBA_Q_SCALE = MOBA_HEAD_DIM ** -0.5 * math.log2(math.e)

BF16_SUBLANES = 16
MOBA_VROWS = MOBA_HEAD_DIM + BF16_SUBLANES

Z_AQ, Z_AV, Z_AO, Z_BK, Z_CX, Z_CG, Z_GA, Z_GB, Z_GC = 0, 1, 2, 3, 4, 5, 6, 8, 10
Z_COLS = 12 * 1024
T_COLS = 3 * 1024
PREP_ROWS = 128


def _params(vmem_mib, *semantics):
    return pltpu.CompilerParams(dimension_semantics=semantics, vmem_limit_bytes=vmem_mib * MIB)


def _resident(block_shape, index_map):
    return pl.BlockSpec(block_shape, index_map, pipeline_mode=pl.Buffered(1))


def _rms_norm(x, g):
    return x * lax.rsqrt(jnp.mean(x * x, axis=-1, keepdims=True) + EPS) * g


def _sigmoid(x):
    return 1.0 / (1.0 + jnp.exp(-x))


def _sigmoid_tanh(x):
    return 0.5 * (jnp.tanh(0.5 * x) + 1.0)


def _log_sigmoid(x):
    return jnp.minimum(x, 0.0) - jnp.log1p(jnp.exp(-jnp.abs(x)))


def _softplus(x):
    return jnp.maximum(x, 0.0) + jnp.log1p(jnp.exp(-jnp.abs(x)))


def _split3(x):
    hi = x.astype(BF16)
    r1 = x - hi.astype(F32)
    mid = r1.astype(BF16)
    lo = (r1 - mid.astype(F32)).astype(BF16)
    return hi, mid, lo


def _ffn_body(x_ref, g_ref, wg_ref, wu_ref, wd_ref, *rest, final):
    if final:
        fg_ref, o_ref, h_s = rest
    else:
        o_ref, h_s = rest
    j = pl.program_id(1)
    tf = wg_ref.shape[1]

    @pl.when(j == 0)
    def _():
        x = x_ref[...]
        h_s[...] = _rms_norm(x, g_ref[...]).astype(BF16)
        o_ref[...] = x

    h = h_s[...]
    valid = D_FF - j * tf
    a = jnp.dot(h, wg_ref[...].astype(BF16), preferred_element_type=F32)
    b = jnp.dot(h, wu_ref[...].astype(BF16), preferred_element_type=F32)
    col = lax.broadcasted_iota(jnp.int32, (1, tf), 1)
    act = jnp.where(col < valid, 0.5 * (a * _sigmoid(a) * b), 0.0).astype(BF16)
    row = lax.broadcasted_iota(jnp.int32, (tf, 1), 0)
    wd = jnp.where(row < valid, wd_ref[...], 0.0).astype(BF16)
    o_ref[...] += jnp.dot(act, wd, preferred_element_type=F32)

    if final:
        @pl.when(j == pl.num_programs(1) - 1)
        def _():
            o_ref[...] = _rms_norm(o_ref[...], fg_ref[...])


def _ffn(x, norm_g, wg, wu, wd, layer, final_g=None):
    s, d = x.shape
    tm = min(FFN_ROW_TILE, s)
    tf = FF_TILE
    final = final_g is not None
    in_specs = [
        _resident((tm, d), lambda i, j: (i, 0)),
        pl.BlockSpec((1, d), lambda i, j: (0, 0)),
        pl.BlockSpec((None, d, tf), lambda i, j: (layer, 0, j)),
        pl.BlockSpec((None, d, tf), lambda i, j: (layer, 0, j)),
        pl.BlockSpec((None, tf, d), lambda i, j: (layer, j, 0)),
    ]
    args = [x, norm_g, wg, wu, wd]
    if final:
        in_specs.append(pl.BlockSpec((1, d), lambda i, j: (0, 0)))
        args.append(final_g)
    return pl.pallas_call(
        functools.partial(_ffn_body, final=final),
        grid=(s // tm, pl.cdiv(D_FF, tf)),
        in_specs=in_specs,
        out_specs=pl.BlockSpec((tm, d), lambda i, j: (i, 0)),
        out_shape=jax.ShapeDtypeStruct((s, d), F32),
        scratch_shapes=[pltpu.VMEM((tm, d), BF16)],
        compiler_params=_params(56, "parallel", "arbitrary"),
        name="ffn",
    )(*args)


def _prep_body(w_ref, main_ref, t_ref, gate_ref):
    w = MLSTM_WIDTH
    b0 = 4 * w + 2 * MLSTM_HEADS
    main_ref[:, 0:w] = w_ref[:, 0:w].astype(BF16)
    main_ref[:, w:3 * w] = w_ref[:, 2 * w:4 * w].astype(BF16)
    main_ref[:, 3 * w:4 * w] = w_ref[:, b0 + w:b0 + 2 * w].astype(BF16)
    main_ref[:, 4 * w:] = w_ref[:, b0 + 3 * w:].astype(BF16)
    t_ref[:, 0:w] = w_ref[:, w:2 * w].astype(BF16)
    t_ref[:, w:2 * w] = w_ref[:, b0:b0 + w].astype(BF16)
    t_ref[:, 2 * w:] = w_ref[:, b0 + 2 * w:b0 + 3 * w].astype(BF16)
    lane = lax.broadcasted_iota(jnp.int32, gate_ref.shape, 1)
    gate_ref[...] = jnp.where(lane < 2 * MLSTM_HEADS, w_ref[:, 4 * w:4 * w + GATE_PAD], 0.0).astype(BF16)


def _prep_w_in(w_in):
    depth, d, n_in = w_in.shape
    rk = PREP_ROWS
    return pl.pallas_call(
        _prep_body,
        grid=(depth, d // rk),
        in_specs=[pl.BlockSpec((None, rk, n_in), lambda l, r: (l, r, 0))],
        out_specs=[
            pl.BlockSpec((None, rk, Z_COLS), lambda l, r: (l, r, 0)),
            pl.BlockSpec((None, rk, T_COLS), lambda l, r: (l, r, 0)),
            pl.BlockSpec((None, rk, GATE_PAD), lambda l, r: (l, r, 0)),
        ],
        out_shape=[
            jax.ShapeDtypeStruct((depth, d, Z_COLS), BF16),
            jax.ShapeDtypeStruct((depth, d, T_COLS), BF16),
            jax.ShapeDtypeStruct((depth, d, GATE_PAD), BF16),
        ],
        compiler_params=_params(48, "parallel", "parallel"),
        name="prep_w_in",
    )(w_in)


def _proj_body(x_ref, g_ref, w_ref, wt_ref, wif_ref,
               z_ref, akt_ref, bqt_ref, bvt_ref, ifc_ref, ifr_ref, h_s):
    @pl.when(pl.program_id(1) == 0)
    def _():
        h_s[...] = _rms_norm(x_ref[...], g_ref[...]).astype(BF16)
        h = h_s[...]
        w = MLSTM_WIDTH
        tm = h.shape[0]
        akt_ref[...] = jnp.dot(h, wt_ref[:, 0:w], preferred_element_type=F32).astype(BF16).T
        bq = jnp.dot(h, wt_ref[:, w:2 * w], preferred_element_type=F32) * MOBA_Q_SCALE
        bqt_ref[...] = bq.astype(BF16).T
        bvt = jnp.dot(h, wt_ref[:, 2 * w:3 * w], preferred_element_type=F32).astype(BF16).T
        ones = jnp.ones((BF16_SUBLANES, tm), BF16)
        dh = MOBA_HEAD_DIM
        bvt = jnp.concatenate(
            [part for hd in range(MOBA_HEADS) for part in (bvt[hd * dh:(hd + 1) * dh, :], ones)], axis=0)
        for n in range(bvt_ref.shape[0]):
            bvt_ref[n] = bvt[:, n * MOBA_BLOCK:(n + 1) * MOBA_BLOCK]
        ifc_ref[...] = jnp.dot(h, wif_ref[...], preferred_element_type=F32)
        ifr_ref[...] = ifc_ref[...].T[0:GATE_ROWS, :]

    z_ref[...] = jnp.dot(h_s[...], w_ref[...], preferred_element_type=F32).astype(BF16)


def _proj(x, norm_g, w_main, w_t, w_if, layer):
    s, d = x.shape
    tm = min(ROW_TILE, s)
    tn = PROJ_COL_TILE
    blk = MOBA_BLOCK
    vrows = MOBA_HEADS * MOBA_VROWS
    return pl.pallas_call(
        _proj_body,
        grid=(s // tm, Z_COLS // tn),
        in_specs=[
            _resident((tm, d), lambda i, j: (i, 0)),
            pl.BlockSpec((1, d), lambda i, j: (0, 0)),
            pl.BlockSpec((None, d, tn), lambda i, j: (layer, 0, j)),
            _resident((None, d, T_COLS), lambda i, j: (layer, 0, 0)),
            pl.BlockSpec((None, d, GATE_PAD), lambda i, j: (layer, 0, 0)),
        ],
        out_specs=[
            pl.BlockSpec((tm, tn), lambda i, j: (i, j)),
            pl.BlockSpec((MLSTM_WIDTH, tm), lambda i, j: (0, i)),
            pl.BlockSpec((MOBA_WIDTH, tm), lambda i, j: (0, i)),
            pl.BlockSpec((tm // blk, vrows, blk), lambda i, j: (i, 0, 0)),
            pl.BlockSpec((tm, GATE_PAD), lambda i, j: (i, 0)),
            pl.BlockSpec((GATE_ROWS, tm), lambda i, j: (0, i)),
        ],
        out_shape=[
            jax.ShapeDtypeStruct((s, Z_COLS), BF16),
            jax.ShapeDtypeStruct((MLSTM_WIDTH, s), BF16),
            jax.ShapeDtypeStruct((MOBA_WIDTH, s), BF16),
            jax.ShapeDtypeStruct((s // blk, vrows, blk), BF16),
            jax.ShapeDtypeStruct((s, GATE_PAD), F32),
            jax.ShapeDtypeStruct((GATE_ROWS, s), F32),
        ],
        scratch_shapes=[pltpu.VMEM((tm, d), BF16)],
        compiler_params=_params(56, "parallel", "arbitrary"),
        name="proj",
    )(x, norm_g, w_main, w_t, w_if)


def _mlstm_body(q_ref, v_ref, o_ref, kt_ref, ifc_ref, ifr_ref, bias_r_ref, bias_c_ref, gain_ref,
                y_ref, ct_s, m_s):
    chunk = q_ref.shape[0]
    dh = MLSTM_HEAD_DIM

    @pl.when(pl.program_id(0) == 0)
    def _():
        ct_s[...] = jnp.zeros_like(ct_s)
        m_s[...] = jnp.zeros_like(m_s)

    gates_c = ifc_ref[...] + bias_r_ref[...]
    gates_r = ifr_ref[...] + bias_c_ref[...]
    row = lax.broadcasted_iota(jnp.int32, (chunk, chunk), 0)
    col = lax.broadcasted_iota(jnp.int32, (chunk, chunk), 1)
    causal = row >= col
    tril = jnp.where(causal, 1.0, 0.0).astype(BF16)
    triu = jnp.where(row <= col, 1.0, 0.0).astype(BF16)
    bcum_c = sum(jnp.dot(tril, p, preferred_element_type=F32) for p in _split3(_log_sigmoid(gates_c)))
    bcum_r = sum(jnp.dot(p, triu, preferred_element_type=F32) for p in _split3(_log_sigmoid(gates_r)))
    ones_col = jnp.where(lax.broadcasted_iota(jnp.int32, (chunk, LANES), 1) == 0, 1.0, 0.0).astype(BF16)
    k_scale = dh ** -0.5

    for hd in range(MLSTM_HEADS):
        sl = slice(hd * dh, (hd + 1) * dh)
        fcol = MLSTM_HEADS + hd
        q = q_ref[:, sl]
        kt = kt_ref[sl, :]
        v_aug = jnp.concatenate([v_ref[:, sl], ones_col], axis=1)
        bc = bcum_c[:, fcol:fcol + 1]
        br = bcum_r[fcol:fcol + 1, :]
        i_r = gates_r[hd:hd + 1, :]
        m_prev = m_s[hd:hd + 1, 0:1]
        ct = ct_s[hd]

        d_intra = jnp.where(causal, bc - br + i_r, -jnp.inf)
        d_inter = bc + m_prev
        m_t = jnp.maximum(jnp.max(d_intra, axis=1, keepdims=True), d_inter)
        w_intra = jnp.exp(d_intra - m_t)
        w_inter = jnp.exp(d_inter - m_t)
        qk = jnp.dot(q, kt, preferred_element_type=F32)
        sc = (qk * k_scale * w_intra).astype(BF16)
        numden = (jnp.dot(sc, v_aug, preferred_element_type=F32)
                  + w_inter * jnp.dot(q, ct.astype(BF16), preferred_element_type=F32))
        num = numden[:, :dh]
        den = numden[:, dh:dh + 1]
        h_out = num / jnp.maximum(jnp.abs(den), jnp.exp(-m_t))
        h_out = h_out * lax.rsqrt(jnp.mean(h_out * h_out, axis=-1, keepdims=True) + EPS)
        h_out = h_out * gain_ref[:, sl]
        y_ref[:, sl] = (_sigmoid(o_ref[:, sl].astype(F32)) * h_out).astype(BF16)

        b_last = bc[chunk - 1:chunk, :]
        g_r = b_last - br + i_r
        m_new = jnp.maximum(b_last + m_prev, jnp.max(g_r, axis=1, keepdims=True))
        decay = jnp.exp(b_last + m_prev - m_new)
        w_r = jnp.exp(g_r - m_new) * k_scale
        ktw = (kt.astype(F32) * w_r).astype(BF16)
        ct_s[hd] = decay * ct + jnp.dot(ktw, v_aug, preferred_element_type=F32)
        m_s[hd:hd + 1, :] = jnp.broadcast_to(m_new, (1, LANES))


def _mlstm(z, kt, ifc, ifr, bias_r, bias_c, gain):
    s = z.shape[0]
    chunk = min(MLSTM_CHUNK, s)
    w = MLSTM_WIDTH
    return pl.pallas_call(
        _mlstm_body,
        grid=(s // chunk,),
        in_specs=[
            pl.BlockSpec((chunk, w), lambda c: (c, Z_AQ)),
            pl.BlockSpec((chunk, w), lambda c: (c, Z_AV)),
            pl.BlockSpec((chunk, w), lambda c: (c, Z_AO)),
            pl.BlockSpec((w, chunk), lambda c: (0, c)),
            pl.BlockSpec((chunk, GATE_PAD), lambda c: (c, 0)),
            pl.BlockSpec((GATE_ROWS, chunk), lambda c: (0, c)),
            pl.BlockSpec((1, GATE_PAD), lambda c: (0, 0)),
            pl.BlockSpec((GATE_ROWS, 1), lambda c: (0, 0)),
            pl.BlockSpec((1, w), lambda c: (0, 0)),
        ],
        out_specs=pl.BlockSpec((chunk, w), lambda c: (c, 0)),
        out_shape=jax.ShapeDtypeStruct((s, w), BF16),
        scratch_shapes=[
            pltpu.VMEM((MLSTM_HEADS, MLSTM_HEAD_DIM, MLSTM_HEAD_DIM + LANES), F32),
            pltpu.VMEM((SUBLANES, LANES), F32),
        ],
        compiler_params=_params(32, "arbitrary"),
        name="mlstm",
    )(z, z, z, kt, ifc, ifr, bias_r, bias_c, gain)


def _moba_body(qt_ref, k_ref, vt_ref, y_ref, kmean_s, qaug_s, acc_s, sc_s):
    blk = MOBA_BLOCK
    dh = MOBA_HEAD_DIM
    vr = MOBA_VROWS
    heads = qt_ref.shape[0] // dh
    n_blk = k_ref.shape[0] // blk
    nbp = kmean_s.shape[0]
    i = pl.program_id(1)

    @pl.when(i == 0)
    def _():
        kmean_s[...] = jnp.zeros_like(kmean_s)

        def mean_block(b, carry):
            kb = k_ref[pl.ds(pl.multiple_of(b * blk, blk), blk), :].astype(F32)
            kmean_s[pl.ds(b, 1), :] = jnp.mean(kb, axis=0, keepdims=True)
            return carry

        lax.fori_loop(0, n_blk, mean_block, 0)

    block_id = lax.broadcasted_iota(jnp.int32, (nbp, blk), 0).astype(F32)
    i_f = i.astype(F32)
    for hd in range(heads):
        hs = slice(hd * dh, (hd + 1) * dh)
        qt = qt_ref[hs, :]
        km = kmean_s[:, hs]
        km_hi = km.astype(BF16)
        km_lo = (km - km_hi.astype(F32)).astype(BF16)
        g = (jnp.dot(km_hi, qt, preferred_element_type=F32)
             + jnp.dot(km_lo, qt, preferred_element_type=F32))
        g = jnp.where(block_id < i_f, g, -jnp.inf)
        bias = jnp.full((nbp, blk), MASK_NEG, F32)
        for _ in range(MOBA_TOPK):
            mx = jnp.max(g, axis=0, keepdims=True)
            first = jnp.min(jnp.where(g == mx, block_id, float(nbp)), axis=0, keepdims=True)
            first = jnp.where(mx > -jnp.inf, first, -1.0)
            pick = block_id == first
            bias = jnp.where(pick, 0.0, bias)
            g = jnp.where(pick, -jnp.inf, g)
        if nbp < LANES:
            bias = jnp.concatenate([bias, jnp.zeros((LANES - nbp, blk), F32)], axis=0)
        qaug_s[hd] = jnp.concatenate([qt, bias.astype(BF16)], axis=0)
        acc_s[hd] = jnp.zeros((vr, blk), F32)

    lane = lax.broadcasted_iota(jnp.int32, (blk, LANES), 1)

    def block_onehot(j):
        return jnp.where(lane == jnp.where(j < i, j, -1), 1.0, 0.0).astype(BF16)

    def block_scores(hd, j, onehot):
        k_j = k_ref[pl.ds(pl.multiple_of(j * blk, blk), blk), hd * dh:(hd + 1) * dh]
        return jnp.dot(jnp.concatenate([k_j, onehot], axis=1), qaug_s[hd], preferred_element_type=F32)

    def attend(hd, m_old, st, j):
        m_new = jnp.maximum(m_old, jnp.max(st, axis=0, keepdims=True))
        alpha = jnp.exp2(m_old - m_new)
        p = jnp.exp2(st - m_new).astype(BF16)
        vt = vt_ref[j, hd * vr:(hd + 1) * vr, :]
        acc_s[hd] = alpha * acc_s[hd] + jnp.dot(vt, p, preferred_element_type=F32)
        return m_new

    first = block_onehot(0)
    for hd in range(heads):
        sc_s[hd] = block_scores(hd, 0, first)

    def past_block(j, maxima):
        onehot = block_onehot(j + 1)
        new_maxima = []
        st = sc_s[0]
        sc_s[0] = block_scores(0, j + 1, onehot)
        for hd in range(heads):
            if hd + 1 < heads:
                st_next = sc_s[hd + 1]
                sc_s[hd + 1] = block_scores(hd + 1, j + 1, onehot)
            new_maxima.append(attend(hd, maxima[hd], st, j))
            st = st_next
        return tuple(new_maxima)

    maxima = lax.fori_loop(0, i, past_block, (jnp.full((1, blk), NEG_INIT, F32),) * heads)

    key_pos = lax.broadcasted_iota(jnp.int32, (blk, blk), 0)
    query_pos = lax.broadcasted_iota(jnp.int32, (blk, blk), 1)
    causal = key_pos <= query_pos
    for hd in range(heads):
        attend(hd, maxima[hd], jnp.where(causal, sc_s[hd], -jnp.inf), i)
        out = acc_s[hd, 0:dh, :] / acc_s[hd, dh:dh + 1, :]
        y_ref[:, hd * dh:(hd + 1) * dh] = out.T.astype(BF16)


def _moba(z, bqt, bvt):
    s = z.shape[0]
    blk = MOBA_BLOCK
    dh = MOBA_HEAD_DIM
    gw = MOBA_HEAD_GROUP * dh
    n_blk = s // blk
    nbp = -(-n_blk // SUBLANES) * SUBLANES
    k_tile = Z_BK * (MOBA_WIDTH // gw)
    return pl.pallas_call(
        _moba_body,
        grid=(MOBA_WIDTH // gw, n_blk),
        in_specs=[
            pl.BlockSpec((gw, blk), lambda g, i: (g, i)),
            _resident((s, gw), lambda g, i: (0, k_tile + g)),
            _resident((n_blk, MOBA_HEAD_GROUP * MOBA_VROWS, blk), lambda g, i: (0, g, 0)),
        ],
        out_specs=pl.BlockSpec((blk, gw), lambda g, i: (i, g)),
        out_shape=jax.ShapeDtypeStruct((s, MOBA_WIDTH), BF16),
        scratch_shapes=[
            pltpu.VMEM((nbp, gw), F32),
            pltpu.VMEM((MOBA_HEAD_GROUP, dh + LANES, blk), BF16),
            pltpu.VMEM((MOBA_HEAD_GROUP, MOBA_VROWS, blk), F32),
            pltpu.VMEM((MOBA_HEAD_GROUP, blk, blk), F32),
        ],
        compiler_params=_params(48, "parallel", "arbitrary"),
        name="moba",
    )(bqt, z, bvt)


def _lru_body(x_ref, gate_ref, cw_ref, cb_ref, wa_ref, ba_ref, wx_ref, bx_ref, lam_ref,
              y_ref, tail_s, hlast_s, a_s, u_s):
    t, c = x_ref.shape

    @pl.when(pl.program_id(0) == 0)
    def _():
        tail_s[...] = jnp.zeros_like(tail_s)
        hlast_s[...] = jnp.zeros_like(hlast_s)

    x = x_ref[...].astype(F32)
    xx = jnp.concatenate([tail_s[...], x], axis=0)
    tail_s[...] = x[t - SUBLANES:, :]
    xc = cb_ref[...] + cw_ref[LRU_CONV - 1:LRU_CONV, :] * x
    for tap in range(LRU_CONV - 1):
        shift = LRU_CONV - 1 - tap
        xc = xc + cw_ref[tap:tap + 1, :] * xx[SUBLANES - shift:SUBLANES - shift + t, :]

    xcb = xc.astype(BF16)
    bd = LRU_BLOCK_DIM
    ra = jnp.concatenate(
        [jnp.dot(xcb[:, n * bd:(n + 1) * bd], wa_ref[n], preferred_element_type=F32) for n in range(LRU_BLOCKS)],
        axis=1)
    rx = jnp.concatenate(
        [jnp.dot(xcb[:, n * bd:(n + 1) * bd], wx_ref[n], preferred_element_type=F32) for n in range(LRU_BLOCKS)],
        axis=1)
    r = _sigmoid_tanh(ra + ba_ref[...])
    gi = _sigmoid_tanh(rx + bx_ref[...])
    log_a = -LRU_C * r * _softplus(-lam_ref[...])
    a = jnp.exp(log_a)
    a_s[...] = a
    s = -jnp.tanh(log_a) * (a * a + 1.0)
    root = jnp.where(s > 0.0, s * lax.rsqrt(s), 0.0)
    u_s[...] = root * (gi * xc)

    sub = lax.broadcasted_iota(jnp.int32, (SUBLANES, c), 0)

    def group(gidx, h_prev):
        r0 = pl.multiple_of(gidx * SUBLANES, SUBLANES)
        a = a_s[pl.ds(r0, SUBLANES), :]
        u = u_s[pl.ds(r0, SUBLANES), :]
        d = 1
        while d < SUBLANES:
            keep = sub >= d
            a_sh = jnp.where(keep, pltpu.roll(a, d, 0), 1.0)
            u_sh = jnp.where(keep, pltpu.roll(u, d, 0), 0.0)
            u = a * u_sh + u
            a = a * a_sh
            d *= 2
        h = u + a * h_prev
        u_s[pl.ds(r0, SUBLANES), :] = h
        return jnp.broadcast_to(h[SUBLANES - 1:SUBLANES, :], (SUBLANES, c))

    h_last = lax.fori_loop(0, t // SUBLANES, group, hlast_s[...], unroll=4)
    hlast_s[...] = h_last

    gt = gate_ref[...].astype(F32)
    gelu = 0.5 * gt * (1.0 + jnp.tanh(0.7978845608028654 * (gt + 0.044715 * (gt * gt * gt))))
    y_ref[...] = (u_s[...] * gelu).astype(BF16)


def _lru(z, conv_w, conv_b, w_a, b_a, w_x, b_x, lam):
    s = z.shape[0]
    c = LRU_WIDTH
    t = min(LRU_TILE, s)
    bd = LRU_BLOCK_DIM
    vec = pl.BlockSpec((1, c), lambda i: (0, 0))
    blockw = pl.BlockSpec((LRU_BLOCKS, bd, bd), lambda i: (0, 0, 0))
    return pl.pallas_call(
        _lru_body,
        grid=(s // t,),
        in_specs=[
            pl.BlockSpec((t, c), lambda i: (i, Z_CX)),
            pl.BlockSpec((t, c), lambda i: (i, Z_CG)),
            pl.BlockSpec((LRU_CONV, c), lambda i: (0, 0)),
            vec, blockw, vec, blockw, vec, vec,
        ],
        out_specs=pl.BlockSpec((t, c), lambda i: (i, 0)),
        out_shape=jax.ShapeDtypeStruct((s, c), BF16),
        scratch_shapes=[
            pltpu.VMEM((SUBLANES, c), F32),
            pltpu.VMEM((SUBLANES, c), F32),
            pltpu.VMEM((t, c), F32),
            pltpu.VMEM((t, c), F32),
        ],
        compiler_params=_params(32, "arbitrary"),
        name="rglru",
    )(z, z, conv_w, conv_b, w_a, b_a, w_x, b_x, lam)


def _merge_body(x_ref, ya_ref, yb_ref, yc_ref, ga_ref, gb_ref, gc_ref, wa_ref, wb_ref, wc_ref, wo_ref, o_ref):
    @pl.when(pl.program_id(1) == 0)
    def _():
        o_ref[...] = x_ref[...]

    merged = (_sigmoid(ga_ref[...].astype(F32)) * jnp.dot(ya_ref[...], wa_ref[...], preferred_element_type=F32)
              + _sigmoid(gb_ref[...].astype(F32)) * jnp.dot(yb_ref[...], wb_ref[...], preferred_element_type=F32)
              + _sigmoid(gc_ref[...].astype(F32)) * jnp.dot(yc_ref[...], wc_ref[...], preferred_element_type=F32))
    o_ref[...] += jnp.dot(merged.astype(BF16), wo_ref[...], preferred_element_type=F32)


def _merge(x, ya, yb, yc, z, wa, wb, wc, wo, layer):
    s, d = x.shape
    tm = min(ROW_TILE, s)
    tn = MERGE_COL_TILE
    w = ya.shape[1]
    per_tile = MLSTM_WIDTH // tn
    ybranch = pl.BlockSpec((tm, w), lambda i, j: (i, 0))
    wup = pl.BlockSpec((None, w, tn), lambda i, j: (layer, 0, j))

    def gate_spec(base):
        return pl.BlockSpec((tm, tn), lambda i, j: (i, base * per_tile + j))

    return pl.pallas_call(
        _merge_body,
        grid=(s // tm, d // tn),
        in_specs=[
            pl.BlockSpec((tm, d), lambda i, j: (i, 0)),
            ybranch, ybranch, ybranch,
            gate_spec(Z_GA), gate_spec(Z_GB), gate_spec(Z_GC),
            wup, wup, wup,
            pl.BlockSpec((None, tn, d), lambda i, j: (layer, j, 0)),
        ],
        out_specs=pl.BlockSpec((tm, d), lambda i, j: (i, 0)),
        out_shape=jax.ShapeDtypeStruct((s, d), F32),
        compiler_params=_params(48, "parallel", "arbitrary"),
        name="merge",
    )(x, ya, yb, yc, z, z, z, wa, wb, wc, wo)


def _row(v):
    return v.reshape(1, -1).astype(F32)


def kernel(x, ffn1_norm, ffn1_w_gate, ffn1_w_up, ffn1_w_down, mix_norm, w_in, mlstm_b_i, mlstm_b_f, mlstm_head_norm, lru_conv_w, lru_conv_b, lru_w_a, lru_b_a, lru_w_x, lru_b_x, lru_lambda, w_up_a, w_up_b, w_up_c, w_out, ffn2_norm, ffn2_w_gate, ffn2_w_up, ffn2_w_down, final_norm):
    b, s, d = x.shape
    depth = w_in.shape[0]
    w_main, w_t, w_if = _prep_w_in(w_in)
    merge_w = tuple(w.astype(BF16) for w in (w_up_a, w_up_b, w_up_c, w_out))
    outs = []
    for bi in range(b):
        xs = x[bi]
        for l in range(depth):
            xs = _ffn(xs, _row(ffn1_norm[l]), ffn1_w_gate, ffn1_w_up, ffn1_w_down, l)
            z, akt, bqt, bvt, ifc, ifr = _proj(xs, _row(mix_norm[l]), w_main, w_t, w_if, l)
            gate_bias = jnp.concatenate([mlstm_b_i[l], mlstm_b_f[l]]).astype(F32)
            bias_r = jnp.pad(gate_bias, (0, GATE_PAD - gate_bias.shape[0])).reshape(1, GATE_PAD)
            bias_c = jnp.pad(gate_bias, (0, GATE_ROWS - gate_bias.shape[0])).reshape(GATE_ROWS, 1)
            ya = _mlstm(z, akt, ifc, ifr, bias_r, bias_c, _row(mlstm_head_norm[l]))
            yb = _moba(z, bqt, bvt)
            yc = _lru(z, lru_conv_w[l].astype(F32), _row(lru_conv_b[l]), lru_w_a[l].astype(BF16), _row(lru_b_a[l]),
                      lru_w_x[l].astype(BF16), _row(lru_b_x[l]), _row(lru_lambda[l]))
            xs = _merge(xs, ya, yb, yc, z, *merge_w, l)
            xs = _ffn(xs, _row(ffn2_norm[l]), ffn2_w_gate, ffn2_w_up, ffn2_w_down, l,
                      final_g=_row(final_norm) if l == depth - 1 else None)
        outs.append(xs)
    return jnp.stack(outs).astype(x.dtype)
```

```python
import functools
import math

import jax
import jax.numpy as jnp
from jax import lax
from jax.experimental import pallas as pl
from jax.experimental.pallas import tpu as pltpu

F32 = jnp.float32
BF16 = jnp.bfloat16

D_MODEL = 2048
EPS = 1e-6
MLSTM_HEADS = 4
MLSTM_WIDTH = D_MODEL // 2
MLSTM_HEAD_DIM = MLSTM_WIDTH // MLSTM_HEADS
MOBA_HEADS = 8
MOBA_WIDTH = D_MODEL // 2
MOBA_HEAD_DIM = MOBA_WIDTH // MOBA_HEADS
MOBA_BLOCK = 256
MOBA_TOPK = 3
LRU_WIDTH = D_MODEL // 2
LRU_BLOCKS = 8
LRU_BLOCK_DIM = LRU_WIDTH // LRU_BLOCKS
LRU_CONV = 4
LRU_C = 8.0
D_FF = ((8 * D_MODEL // 3 + 127) // 128) * 128

LANES = 128
SUBLANES = 8
MIB = 1024 * 1024

ROW_TILE = 512
FFN_ROW_TILE = 1024
FF_TILE = 256
PROJ_COL_TILE = 2048
MERGE_COL_TILE = 512
MLSTM_CHUNK = 256
LRU_TILE = 512
MOBA_HEAD_GROUP = 8
GATE_PAD = LANES
GATE_ROWS = 16
MASK_NEG = -1e30
NEG_INIT = -1e30
MOBA_Q_SCALE = MOBA_HEAD_DIM ** -0.5 * math.log2(math.e)

BF16_SUBLANES = 16
MOBA_VROWS = MOBA_HEAD_DIM + BF16_SUBLANES

Z_AQ, Z_AV, Z_AO, Z_BK, Z_CX, Z_CG, Z_GA, Z_GB, Z_GC = 0, 1, 2, 3, 4, 5, 6, 8, 10
Z_COLS = 12 * 1024
T_COLS = 3 * 1024
PREP_ROWS = 128


def _params(vmem_mib, *semantics):
    return pltpu.CompilerParams(dimension_semantics=semantics, vmem_limit_bytes=vmem_mib * MIB)


def _resident(block_shape, index_map):
    return pl.BlockSpec(block_shape, index_map, pipeline_mode=pl.Buffered(1))


def _rms_norm(x, g):
    return x * lax.rsqrt(jnp.mean(x * x, axis=-1, keepdims=True) + EPS) * g


def _sigmoid(x):
    return 1.0 / (1.0 + jnp.exp(-x))


def _sigmoid_tanh(x):
    return 0.5 * (jnp.tanh(0.5 * x) + 1.0)


def _log_sigmoid(x):
    return jnp.minimum(x, 0.0) - jnp.log1p(jnp.exp(-jnp.abs(x)))


def _softplus(x):
    return jnp.maximum(x, 0.0) + jnp.log1p(jnp.exp(-jnp.abs(x)))


def _split3(x):
    hi = x.astype(BF16)
    r1 = x - hi.astype(F32)
    mid = r1.astype(BF16)
    lo = (r1 - mid.astype(F32)).astype(BF16)
    return hi, mid, lo


def _ffn_body(x_ref, g_ref, wg_ref, wu_ref, wd_ref, *rest, final):
    if final:
        fg_ref, o_ref, h_s = rest
    else:
        o_ref, h_s = rest
    j = pl.program_id(1)
    tf = wg_ref.shape[1]

    @pl.when(j == 0)
    def _():
        x = x_ref[...]
        h_s[...] = _rms_norm(x, g_ref[...]).astype(BF16)
        o_ref[...] = x

    h = h_s[...]
    valid = D_FF - j * tf
    a = jnp.dot(h, wg_ref[...].astype(BF16), preferred_element_type=F32)
    b = jnp.dot(h, wu_ref[...].astype(BF16), preferred_element_type=F32)
    col = lax.broadcasted_iota(jnp.int32, (1, tf), 1)
    act = jnp.where(col < valid, 0.5 * (a * _sigmoid(a) * b), 0.0).astype(BF16)
    row = lax.broadcasted_iota(jnp.int32, (tf, 1), 0)
    wd = jnp.where(row < valid, wd_ref[...], 0.0).astype(BF16)
    o_ref[...] += jnp.dot(act, wd, preferred_element_type=F32)

    if final:
        @pl.when(j == pl.num_programs(1) - 1)
        def _():
            o_ref[...] = _rms_norm(o_ref[...], fg_ref[...])


def _ffn(x, norm_g, wg, wu, wd, layer, final_g=None):
    s, d = x.shape
    tm = min(FFN_ROW_TILE, s)
    tf = FF_TILE
    final = final_g is not None
    in_specs = [
        _resident((tm, d), lambda i, j: (i, 0)),
        pl.BlockSpec((1, d), lambda i, j: (0, 0)),
        pl.BlockSpec((None, d, tf), lambda i, j: (layer, 0, j)),
        pl.BlockSpec((None, d, tf), lambda i, j: (layer, 0, j)),
        pl.BlockSpec((None, tf, d), lambda i, j: (layer, j, 0)),
    ]
    args = [x, norm_g, wg, wu, wd]
    if final:
        in_specs.append(pl.BlockSpec((1, d), lambda i, j: (0, 0)))
        args.append(final_g)
    return pl.pallas_call(
        functools.partial(_ffn_body, final=final),
        grid=(s // tm, pl.cdiv(D_FF, tf)),
        in_specs=in_specs,
        out_specs=pl.BlockSpec((tm, d), lambda i, j: (i, 0)),
        out_shape=jax.ShapeDtypeStruct((s, d), F32),
        scratch_shapes=[pltpu.VMEM((tm, d), BF16)],
        compiler_params=_params(56, "parallel", "arbitrary"),
        name="ffn",
    )(*args)


def _prep_body(wt_ref, main_ref, t_ref, gate_ref):
    w = MLSTM_WIDTH
    b0 = 4 * w + 2 * MLSTM_HEADS

    def piece(first, count):
        return wt_ref[first:first + count, :].T.astype(BF16)

    main_ref[:, 0:w] = piece(0, w)
    main_ref[:, w:3 * w] = piece(2 * w, 2 * w)
    main_ref[:, 3 * w:4 * w] = piece(b0 + w, w)
    main_ref[:, 4 * w:] = piece(b0 + 3 * w, 8 * w)
    t_ref[:, 0:w] = piece(w, w)
    t_ref[:, w:2 * w] = piece(b0, w)
    t_ref[:, 2 * w:] = piece(b0 + 2 * w, w)
    lane = lax.broadcasted_iota(jnp.int32, gate_ref.shape, 1)
    gate_ref[...] = jnp.where(lane < 2 * MLSTM_HEADS, wt_ref[4 * w:4 * w + GATE_PAD, :].T, 0.0).astype(BF16)


def _prep_w_in(w_in):
    depth, d, n_in = w_in.shape
    rk = PREP_ROWS
    return pl.pallas_call(
        _prep_body,
        grid=(depth, d // rk),
        in_specs=[pl.BlockSpec((None, n_in, rk), lambda l, r: (l, 0, r))],
        out_specs=[
            pl.BlockSpec((None, rk, Z_COLS), lambda l, r: (l, r, 0)),
            pl.BlockSpec((None, rk, T_COLS), lambda l, r: (l, r, 0)),
            pl.BlockSpec((None, rk, GATE_PAD), lambda l, r: (l, r, 0)),
        ],
        out_shape=[
            jax.ShapeDtypeStruct((depth, d, Z_COLS), BF16),
            jax.ShapeDtypeStruct((depth, d, T_COLS), BF16),
            jax.ShapeDtypeStruct((depth, d, GATE_PAD), BF16),
        ],
        compiler_params=_params(48, "parallel", "parallel"),
        name="prep_w_in",
    )(jnp.swapaxes(w_in, 1, 2))


def _proj_body(x_ref, g_ref, w_ref, wt_ref, wif_ref,
               z_ref, akt_ref, bqt_ref, bvt_ref, ifc_ref, ifr_ref, h_s):
    @pl.when(pl.program_id(1) == 0)
    def _():
        h_s[...] = _rms_norm(x_ref[...], g_ref[...]).astype(BF16)
        h = h_s[...]
        w = MLSTM_WIDTH
        tm = h.shape[0]
        akt_ref[...] = jnp.dot(h, wt_ref[:, 0:w], preferred_element_type=F32).astype(BF16).T
        bq = jnp.dot(h, wt_ref[:, w:2 * w], preferred_element_type=F32) * MOBA_Q_SCALE
        bqt_ref[...] = bq.astype(BF16).T
        bvt = jnp.dot(h, wt_ref[:, 2 * w:3 * w], preferred_element_type=F32).astype(BF16).T
        ones = jnp.ones((BF16_SUBLANES, tm), BF16)
        dh = MOBA_HEAD_DIM
        bvt = jnp.concatenate(
            [part for hd in range(MOBA_HEADS) for part in (bvt[hd * dh:(hd + 1) * dh, :], ones)], axis=0)
        for n in range(bvt_ref.shape[0]):
            bvt_ref[n] = bvt[:, n * MOBA_BLOCK:(n + 1) * MOBA_BLOCK]
        ifc_ref[...] = jnp.dot(h, wif_ref[...], preferred_element_type=F32)
        ifr_ref[...] = ifc_ref[...].T[0:GATE_ROWS, :]

    z_ref[...] = jnp.dot(h_s[...], w_ref[...], preferred_element_type=F32).astype(BF16)


def _proj(x, norm_g, w_main, w_t, w_if, layer):
    s, d = x.shape
    tm = min(ROW_TILE, s)
    tn = PROJ_COL_TILE
    blk = MOBA_BLOCK
    vrows = MOBA_HEADS * MOBA_VROWS
    return pl.pallas_call(
        _proj_body,
        grid=(s // tm, Z_COLS // tn),
        in_specs=[
            _resident((tm, d), lambda i, j: (i, 0)),
            pl.BlockSpec((1, d), lambda i, j: (0, 0)),
            pl.BlockSpec((None, d, tn), lambda i, j: (layer, 0, j)),
            _resident((None, d, T_COLS), lambda i, j: (layer, 0, 0)),
            pl.BlockSpec((None, d, GATE_PAD), lambda i, j: (layer, 0, 0)),
        ],
        out_specs=[
            pl.BlockSpec((tm, tn), lambda i, j: (i, j)),
            pl.BlockSpec((MLSTM_WIDTH, tm), lambda i, j: (0, i)),
            pl.BlockSpec((MOBA_WIDTH, tm), lambda i, j: (0, i)),
            pl.BlockSpec((tm // blk, vrows, blk), lambda i, j: (i, 0, 0)),
            pl.BlockSpec((tm, GATE_PAD), lambda i, j: (i, 0)),
            pl.BlockSpec((GATE_ROWS, tm), lambda i, j: (0, i)),
        ],
        out_shape=[
            jax.ShapeDtypeStruct((s, Z_COLS), BF16),
            jax.ShapeDtypeStruct((MLSTM_WIDTH, s), BF16),
            jax.ShapeDtypeStruct((MOBA_WIDTH, s), BF16),
            jax.ShapeDtypeStruct((s // blk, vrows, blk), BF16),
            jax.ShapeDtypeStruct((s, GATE_PAD), F32),
            jax.ShapeDtypeStruct((GATE_ROWS, s), F32),
        ],
        scratch_shapes=[pltpu.VMEM((tm, d), BF16)],
        compiler_params=_params(56, "parallel", "arbitrary"),
        name="proj",
    )(x, norm_g, w_main, w_t, w_if)


def _mlstm_body(q_ref, v_ref, o_ref, kt_ref, ifc_ref, ifr_ref, bias_r_ref, bias_c_ref, gain_ref,
                y_ref, ct_s, m_s):
    chunk = q_ref.shape[0]
    dh = MLSTM_HEAD_DIM

    @pl.when(pl.program_id(0) == 0)
    def _():
        ct_s[...] = jnp.zeros_like(ct_s)
        m_s[...] = jnp.zeros_like(m_s)

    gates_c = ifc_ref[...] + bias_r_ref[...]
    gates_r = ifr_ref[...] + bias_c_ref[...]
    row = lax.broadcasted_iota(jnp.int32, (chunk, chunk), 0)
    col = lax.broadcasted_iota(jnp.int32, (chunk, chunk), 1)
    causal = row >= col
    tril = jnp.where(causal, 1.0, 0.0).astype(BF16)
    triu = jnp.where(row <= col, 1.0, 0.0).astype(BF16)
    bcum_c = sum(jnp.dot(tril, p, preferred_element_type=F32) for p in _split3(_log_sigmoid(gates_c)))
    bcum_r = sum(jnp.dot(p, triu, preferred_element_type=F32) for p in _split3(_log_sigmoid(gates_r)))
    ones_col = jnp.where(lax.broadcasted_iota(jnp.int32, (chunk, LANES), 1) == 0, 1.0, 0.0).astype(BF16)
    k_scale = dh ** -0.5

    for hd in range(MLSTM_HEADS):
        sl = slice(hd * dh, (hd + 1) * dh)
        fcol = MLSTM_HEADS + hd
        q = q_ref[:, sl]
        kt = kt_ref[sl, :]
        v_aug = jnp.concatenate([v_ref[:, sl], ones_col], axis=1)
        bc = bcum_c[:, fcol:fcol + 1]
        br = bcum_r[fcol:fcol + 1, :]
        i_r = gates_r[hd:hd + 1, :]
        m_prev = m_s[hd:hd + 1, 0:1]
        ct = ct_s[hd]

        d_intra = jnp.where(causal, bc - br + i_r, -jnp.inf)
        d_inter = bc + m_prev
        m_t = jnp.maximum(jnp.max(d_intra, axis=1, keepdims=True), d_inter)
        w_intra = jnp.exp(d_intra - m_t)
        w_inter = jnp.exp(d_inter - m_t)
        qk = jnp.dot(q, kt, preferred_element_type=F32)
        sc = (qk * k_scale * w_intra).astype(BF16)
        numden = (jnp.dot(sc, v_aug, preferred_element_type=F32)
                  + w_inter * jnp.dot(q, ct.astype(BF16), preferred_element_type=F32))
        num = numden[:, :dh]
        den = numden[:, dh:dh + 1]
        h_out = num / jnp.maximum(jnp.abs(den), jnp.exp(-m_t))
        h_out = h_out * lax.rsqrt(jnp.mean(h_out * h_out, axis=-1, keepdims=True) + EPS)
        h_out = h_out * gain_ref[:, sl]
        y_ref[:, sl] = (_sigmoid(o_ref[:, sl].astype(F32)) * h_out).astype(BF16)

        b_last = bc[chunk - 1:chunk, :]
        g_r = b_last - br + i_r
        m_new = jnp.maximum(b_last + m_prev, jnp.max(g_r, axis=1, keepdims=True))
        decay = jnp.exp(b_last + m_prev - m_new)
        w_r = jnp.exp(g_r - m_new) * k_scale
        ktw = (kt.astype(F32) * w_r).astype(BF16)
        ct_s[hd] = decay * ct + jnp.dot(ktw, v_aug, preferred_element_type=F32)
        m_s[hd:hd + 1, :] = jnp.broadcast_to(m_new, (1, LANES))


def _mlstm(z, kt, ifc, ifr, bias_r, bias_c, gain):
    s = z.shape[0]
    chunk = min(MLSTM_CHUNK, s)
    w = MLSTM_WIDTH
    return pl.pallas_call(
        _mlstm_body,
        grid=(s // chunk,),
        in_specs=[
            pl.BlockSpec((chunk, w), lambda c: (c, Z_AQ)),
            pl.BlockSpec((chunk, w), lambda c: (c, Z_AV)),
            pl.BlockSpec((chunk, w), lambda c: (c, Z_AO)),
            pl.BlockSpec((w, chunk), lambda c: (0, c)),
            pl.BlockSpec((chunk, GATE_PAD), lambda c: (c, 0)),
            pl.BlockSpec((GATE_ROWS, chunk), lambda c: (0, c)),
            pl.BlockSpec((1, GATE_PAD), lambda c: (0, 0)),
            pl.BlockSpec((GATE_ROWS, 1), lambda c: (0, 0)),
            pl.BlockSpec((1, w), lambda c: (0, 0)),
        ],
        out_specs=pl.BlockSpec((chunk, w), lambda c: (c, 0)),
        out_shape=jax.ShapeDtypeStruct((s, w), BF16),
        scratch_shapes=[
            pltpu.VMEM((MLSTM_HEADS, MLSTM_HEAD_DIM, MLSTM_HEAD_DIM + LANES), F32),
            pltpu.VMEM((SUBLANES, LANES), F32),
        ],
        compiler_params=_params(32, "arbitrary"),
        name="mlstm",
    )(z, z, z, kt, ifc, ifr, bias_r, bias_c, gain)


def _moba_body(qt_ref, k_ref, vt_ref, y_ref, kmean_s, qaug_s, acc_s, sc_s):
    blk = MOBA_BLOCK
    dh = MOBA_HEAD_DIM
    vr = MOBA_VROWS
    heads = qt_ref.shape[0] // dh
    n_blk = k_ref.shape[0] // blk
    nbp = kmean_s.shape[0]
    i = pl.program_id(1)

    @pl.when(i == 0)
    def _():
        kmean_s[...] = jnp.zeros_like(kmean_s)

        def mean_block(b, carry):
            kb = k_ref[pl.ds(pl.multiple_of(b * blk, blk), blk), :].astype(F32)
            kmean_s[pl.ds(b, 1), :] = jnp.mean(kb, axis=0, keepdims=True)
            return carry

        lax.fori_loop(0, n_blk, mean_block, 0)

    block_id = lax.broadcasted_iota(jnp.int32, (nbp, blk), 0).astype(F32)
    i_f = i.astype(F32)
    for hd in range(heads):
        hs = slice(hd * dh, (hd + 1) * dh)
        qt = qt_ref[hs, :]
        km = kmean_s[:, hs]
        km_hi = km.astype(BF16)
        km_lo = (km - km_hi.astype(F32)).astype(BF16)
        g = (jnp.dot(km_hi, qt, preferred_element_type=F32)
             + jnp.dot(km_lo, qt, preferred_element_type=F32))
        g = jnp.where(block_id < i_f, g, -jnp.inf)
        bias = jnp.full((nbp, blk), MASK_NEG, F32)
        for _ in range(MOBA_TOPK):
            mx = jnp.max(g, axis=0, keepdims=True)
            first = jnp.min(jnp.where(g == mx, block_id, float(nbp)), axis=0, keepdims=True)
            first = jnp.where(mx > -jnp.inf, first, -1.0)
            pick = block_id == first
            bias = jnp.where(pick, 0.0, bias)
            g = jnp.where(pick, -jnp.inf, g)
        if nbp < LANES:
            bias = jnp.concatenate([bias, jnp.zeros((LANES - nbp, blk), F32)], axis=0)
        qaug_s[hd] = jnp.concatenate([qt, bias.astype(BF16)], axis=0)
        acc_s[hd] = jnp.zeros((vr, blk), F32)

    lane = lax.broadcasted_iota(jnp.int32, (blk, LANES), 1)

    def block_onehot(j):
        return jnp.where(lane == jnp.where(j < i, j, -1), 1.0, 0.0).astype(BF16)

    def block_scores(hd, j, onehot):
        k_j = k_ref[pl.ds(pl.multiple_of(j * blk, blk), blk), hd * dh:(hd + 1) * dh]
        return jnp.dot(jnp.concatenate([k_j, onehot], axis=1), qaug_s[hd], preferred_element_type=F32)

    def attend(hd, m_old, st, j):
        m_new = jnp.maximum(m_old, jnp.max(st, axis=0, keepdims=True))
        alpha = jnp.exp2(m_old - m_new)
        p = jnp.exp2(st - m_new).astype(BF16)
        vt = vt_ref[j, hd * vr:(hd + 1) * vr, :]
        acc_s[hd] = alpha * acc_s[hd] + jnp.dot(vt, p, preferred_element_type=F32)
        return m_new

    first = block_onehot(0)
    for hd in range(heads):
        sc_s[hd] = block_scores(hd, 0, first)

    def past_block(j, maxima):
        onehot = block_onehot(j + 1)
        new_maxima = []
        st = sc_s[0]
        sc_s[0] = block_scores(0, j + 1, onehot)
        for hd in range(heads):
            if hd + 1 < heads:
                st_next = sc_s[hd + 1]
                sc_s[hd + 1] = block_scores(hd + 1, j + 1, onehot)
            new_maxima.append(attend(hd, maxima[hd], st, j))
            st = st_next
        return tuple(new_maxima)

    init = (jnp.full((1, blk), NEG_INIT, F32),) * heads
    maxima = lax.fori_loop(0, i // 2, lambda jj, m: past_block(2 * jj + 1, past_block(2 * jj, m)), init)
    maxima = lax.fori_loop(0, i % 2, lambda _, m: past_block(i - 1, m), maxima)

    key_pos = lax.broadcasted_iota(jnp.int32, (blk, blk), 0)
    query_pos = lax.broadcasted_iota(jnp.int32, (blk, blk), 1)
    causal = key_pos <= query_pos
    for hd in range(heads):
        attend(hd, maxima[hd], jnp.where(causal, sc_s[hd], -jnp.inf), i)
        out = acc_s[hd, 0:dh, :] / acc_s[hd, dh:dh + 1, :]
        y_ref[:, hd * dh:(hd + 1) * dh] = out.T.astype(BF16)


def _moba(z, bqt, bvt):
    s = z.shape[0]
    blk = MOBA_BLOCK
    dh = MOBA_HEAD_DIM
    gw = MOBA_HEAD_GROUP * dh
    n_blk = s // blk
    nbp = -(-n_blk // SUBLANES) * SUBLANES
    k_tile = Z_BK * (MOBA_WIDTH // gw)
    return pl.pallas_call(
        _moba_body,
        grid=(MOBA_WIDTH // gw, n_blk),
        in_specs=[
            pl.BlockSpec((gw, blk), lambda g, i: (g, i)),
            _resident((s, gw), lambda g, i: (0, k_tile + g)),
            _resident((n_blk, MOBA_HEAD_GROUP * MOBA_VROWS, blk), lambda g, i: (0, g, 0)),
        ],
        out_specs=pl.BlockSpec((blk, gw), lambda g, i: (i, g)),
        out_shape=jax.ShapeDtypeStruct((s, MOBA_WIDTH), BF16),
        scratch_shapes=[
            pltpu.VMEM((nbp, gw), F32),
            pltpu.VMEM((MOBA_HEAD_GROUP, dh + LANES, blk), BF16),
            pltpu.VMEM((MOBA_HEAD_GROUP, MOBA_VROWS, blk), F32),
            pltpu.VMEM((MOBA_HEAD_GROUP, blk, blk), F32),
        ],
        compiler_params=_params(48, "parallel", "arbitrary"),
        name="moba",
    )(bqt, z, bvt)


def _lru_body(x_ref, gate_ref, cw_ref, cb_ref, wa_ref, ba_ref, wx_ref, bx_ref, lam_ref,
              y_ref, tail_s, hlast_s, a_s, u_s):
    t, c = x_ref.shape

    @pl.when(pl.program_id(0) == 0)
    def _():
        tail_s[...] = jnp.zeros_like(tail_s)
        hlast_s[...] = jnp.zeros_like(hlast_s)

    x = x_ref[...].astype(F32)
    xx = jnp.concatenate([tail_s[...], x], axis=0)
    tail_s[...] = x[t - SUBLANES:, :]
    xc = cb_ref[...] + cw_ref[LRU_CONV - 1:LRU_CONV, :] * x
    for tap in range(LRU_CONV - 1):
        shift = LRU_CONV - 1 - tap
        xc = xc + cw_ref[tap:tap + 1, :] * xx[SUBLANES - shift:SUBLANES - shift + t, :]

    xcb = xc.astype(BF16)
    bd = LRU_BLOCK_DIM
    ra = jnp.concatenate(
        [jnp.dot(xcb[:, n * bd:(n + 1) * bd], wa_ref[n], preferred_element_type=F32) for n in range(LRU_BLOCKS)],
        axis=1)
    rx = jnp.concatenate(
        [jnp.dot(xcb[:, n * bd:(n + 1) * bd], wx_ref[n], preferred_element_type=F32) for n in range(LRU_BLOCKS)],
        axis=1)
    r = _sigmoid_tanh(ra + ba_ref[...])
    gi = _sigmoid_tanh(rx + bx_ref[...])
    log_a = -LRU_C * r * _softplus(-lam_ref[...])
    a = jnp.exp(log_a)
    a_s[...] = a
    s = -jnp.tanh(log_a) * (a * a + 1.0)
    root = jnp.where(s > 0.0, s * lax.rsqrt(s), 0.0)
    u_s[...] = root * (gi * xc)

    sub = lax.broadcasted_iota(jnp.int32, (SUBLANES, c), 0)

    def group(gidx, h_prev):
        r0 = pl.multiple_of(gidx * SUBLANES, SUBLANES)
        a = a_s[pl.ds(r0, SUBLANES), :]
        u = u_s[pl.ds(r0, SUBLANES), :]
        d = 1
        while d < SUBLANES:
            keep = sub >= d
            a_sh = jnp.where(keep, pltpu.roll(a, d, 0), 1.0)
            u_sh = jnp.where(keep, pltpu.roll(u, d, 0), 0.0)
            u = a * u_sh + u
            a = a * a_sh
            d *= 2
        h = u + a * h_prev
        u_s[pl.ds(r0, SUBLANES), :] = h
        return jnp.broadcast_to(h[SUBLANES - 1:SUBLANES, :], (SUBLANES, c))

    h_last = lax.fori_loop(0, t // SUBLANES, group, hlast_s[...], unroll=4)
    hlast_s[...] = h_last

    gt = gate_ref[...].astype(F32)
    gelu = 0.5 * gt * (1.0 + jnp.tanh(0.7978845608028654 * (gt + 0.044715 * (gt * gt * gt))))
    y_ref[...] = (u_s[...] * gelu).astype(BF16)


def _lru(z, conv_w, conv_b, w_a, b_a, w_x, b_x, lam):
    s = z.shape[0]
    c = LRU_WIDTH
    t = min(LRU_TILE, s)
    bd = LRU_BLOCK_DIM
    vec = pl.BlockSpec((1, c), lambda i: (0, 0))
    blockw = pl.BlockSpec((LRU_BLOCKS, bd, bd), lambda i: (0, 0, 0))
    return pl.pallas_call(
        _lru_body,
        grid=(s // t,),
        in_specs=[
            pl.BlockSpec((t, c), lambda i: (i, Z_CX)),
            pl.BlockSpec((t, c), lambda i: (i, Z_CG)),
            pl.BlockSpec((LRU_CONV, c), lambda i: (0, 0)),
            vec, blockw, vec, blockw, vec, vec,
        ],
        out_specs=pl.BlockSpec((t, c), lambda i: (i, 0)),
        out_shape=jax.ShapeDtypeStruct((s, c), BF16),
        scratch_shapes=[
            pltpu.VMEM((SUBLANES, c), F32),
            pltpu.VMEM((SUBLANES, c), F32),
            pltpu.VMEM((t, c), F32),
            pltpu.VMEM((t, c), F32),
        ],
        compiler_params=_params(32, "arbitrary"),
        name="rglru",
    )(z, z, conv_w, conv_b, w_a, b_a, w_x, b_x, lam)


def _merge_body(x_ref, ya_ref, yb_ref, yc_ref, ga_ref, gb_ref, gc_ref, wa_ref, wb_ref, wc_ref, wo_ref, o_ref):
    @pl.when(pl.program_id(1) == 0)
    def _():
        o_ref[...] = x_ref[...]

    merged = (_sigmoid(ga_ref[...].astype(F32)) * jnp.dot(ya_ref[...], wa_ref[...], preferred_element_type=F32)
              + _sigmoid(gb_ref[...].astype(F32)) * jnp.dot(yb_ref[...], wb_ref[...], preferred_element_type=F32)
              + _sigmoid(gc_ref[...].astype(F32)) * jnp.dot(yc_ref[...], wc_ref[...], preferred_element_type=F32))
    o_ref[...] += jnp.dot(merged.astype(BF16), wo_ref[...], preferred_element_type=F32)


def _merge(x, ya, yb, yc, z, wa, wb, wc, wo, layer):
    s, d = x.shape
    tm = min(ROW_TILE, s)
    tn = MERGE_COL_TILE
    w = ya.shape[1]
    per_tile = MLSTM_WIDTH // tn
    ybranch = pl.BlockSpec((tm, w), lambda i, j: (i, 0))
    wup = pl.BlockSpec((None, w, tn), lambda i, j: (layer, 0, j))

    def gate_spec(base):
        return pl.BlockSpec((tm, tn), lambda i, j: (i, base * per_tile + j))

    return pl.pallas_call(
        _merge_body,
        grid=(s // tm, d // tn),
        in_specs=[
            pl.BlockSpec((tm, d), lambda i, j: (i, 0)),
            ybranch, ybranch, ybranch,
            gate_spec(Z_GA), gate_spec(Z_GB), gate_spec(Z_GC),
            wup, wup, wup,
            pl.BlockSpec((None, tn, d), lambda i, j: (layer, j, 0)),
        ],
        out_specs=pl.BlockSpec((tm, d), lambda i, j: (i, 0)),
        out_shape=jax.ShapeDtypeStruct((s, d), F32),
        compiler_params=_params(48, "parallel", "arbitrary"),
        name="merge",
    )(x, ya, yb, yc, z, z, z, wa, wb, wc, wo)


def _row(v):
    return v.reshape(1, -1).astype(F32)


def kernel(x, ffn1_norm, ffn1_w_gate, ffn1_w_up, ffn1_w_down, mix_norm, w_in, mlstm_b_i, mlstm_b_f, mlstm_head_norm, lru_conv_w, lru_conv_b, lru_w_a, lru_b_a, lru_w_x, lru_b_x, lru_lambda, w_up_a, w_up_b, w_up_c, w_out, ffn2_norm, ffn2_w_gate, ffn2_w_up, ffn2_w_down, final_norm):
    b, s, d = x.shape
    depth = w_in.shape[0]
    w_main, w_t, w_if = _prep_w_in(w_in)
    merge_w = tuple(w.astype(BF16) for w in (w_up_a, w_up_b, w_up_c, w_out))
    outs = []
    for bi in range(b):
        xs = x[bi]
        for l in range(depth):
            xs = _ffn(xs, _row(ffn1_norm[l]), ffn1_w_gate, ffn1_w_up, ffn1_w_down, l)
            z, akt, bqt, bvt, ifc, ifr = _proj(xs, _row(mix_norm[l]), w_main, w_t, w_if, l)
            gate_bias = jnp.concatenate([mlstm_b_i[l], mlstm_b_f[l]]).astype(F32)
            bias_r = jnp.pad(gate_bias, (0, GATE_PAD - gate_bias.shape[0])).reshape(1, GATE_PAD)
            bias_c = jnp.pad(gate_bias, (0, GATE_ROWS - gate_bias.shape[0])).reshape(GATE_ROWS, 1)
            ya = _mlstm(z, akt, ifc, ifr, bias_r, bias_c, _row(mlstm_head_norm[l]))
            yb = _moba(z, bqt, bvt)
            yc = _lru(z, lru_conv_w[l].astype(F32), _row(lru_conv_b[l]), lru_w_a[l].astype(BF16), _row(lru_b_a[l]),
                      lru_w_x[l].astype(BF16), _row(lru_b_x[l]), _row(lru_lambda[l]))
            xs = _merge(xs, ya, yb, yc, z, *merge_w, l)
            xs = _ffn(xs, _row(ffn2_norm[l]), ffn2_w_gate, ffn2_w_up, ffn2_w_down, l,
                      final_g=_row(final_norm) if l == depth - 1 else None)
        outs.append(xs)
    return jnp.stack(outs).astype(x.dtype)
```

```python
import functools
import math

import jax
import jax.numpy as jnp
from jax import lax
from jax.experimental import pallas as pl
from jax.experimental.pallas import tpu as pltpu

F32 = jnp.float32
BF16 = jnp.bfloat16

D_MODEL = 2048
EPS = 1e-6
MLSTM_HEADS = 4
MLSTM_WIDTH = D_MODEL // 2
MLSTM_HEAD_DIM = MLSTM_WIDTH // MLSTM_HEADS
MOBA_HEADS = 8
MOBA_WIDTH = D_MODEL // 2
MOBA_HEAD_DIM = MOBA_WIDTH // MOBA_HEADS
MOBA_BLOCK = 256
MOBA_TOPK = 3
LRU_WIDTH = D_MODEL // 2
LRU_BLOCKS = 8
LRU_BLOCK_DIM = LRU_WIDTH // LRU_BLOCKS
LRU_CONV = 4
LRU_C = 8.0
D_FF = ((8 * D_MODEL // 3 + 127) // 128) * 128

LANES = 128
SUBLANES = 8
MIB = 1024 * 1024

ROW_TILE = 512
FFN_ROW_TILE = 1024
FF_TILE = 256
PROJ_COL_TILE = 2048
MERGE_COL_TILE = 512
MLSTM_CHUNK = 256
LRU_TILE = 512
MOBA_HEAD_GROUP = 8
MOBA_LOOP_BLOCKS = 4
GATE_PAD = LANES
GATE_ROWS = 16
MASK_NEG = -1e30
NEG_INIT = -1e30
MOBA_Q_SCALE = MOBA_HEAD_DIM ** -0.5 * math.log2(math.e)

BF16_SUBLANES = 16
MOBA_VROWS = MOBA_HEAD_DIM + BF16_SUBLANES

Z_AQ, Z_AV, Z_AO, Z_BK, Z_CX, Z_CG, Z_GA, Z_GB, Z_GC = 0, 1, 2, 3, 4, 5, 6, 8, 10
Z_COLS = 12 * 1024
T_COLS = 3 * 1024
PREP_ROWS = 128


def _params(vmem_mib, *semantics):
    return pltpu.CompilerParams(dimension_semantics=semantics, vmem_limit_bytes=vmem_mib * MIB)


def _resident(block_shape, index_map):
    return pl.BlockSpec(block_shape, index_map, pipeline_mode=pl.Buffered(1))


def _rms_norm(x, g):
    return x * lax.rsqrt(jnp.mean(x * x, axis=-1, keepdims=True) + EPS) * g


def _sigmoid(x):
    return 1.0 / (1.0 + jnp.exp(-x))


def _sigmoid_tanh(x):
    return 0.5 * (jnp.tanh(0.5 * x) + 1.0)


def _log_sigmoid(x):
    return jnp.minimum(x, 0.0) - jnp.log1p(jnp.exp(-jnp.abs(x)))


def _softplus(x):
    return jnp.maximum(x, 0.0) + jnp.log1p(jnp.exp(-jnp.abs(x)))


def _split3(x):
    hi = x.astype(BF16)
    r1 = x - hi.astype(F32)
    mid = r1.astype(BF16)
    lo = (r1 - mid.astype(F32)).astype(BF16)
    return hi, mid, lo


def _ffn_body(x_ref, g_ref, wg_ref, wu_ref, wd_ref, *rest, final):
    if final:
        fg_ref, o_ref, h_s = rest
    else:
        o_ref, h_s = rest
    j = pl.program_id(1)
    tf = wg_ref.shape[1]

    @pl.when(j == 0)
    def _():
        x = x_ref[...]
        h_s[...] = _rms_norm(x, g_ref[...]).astype(BF16)
        o_ref[...] = x

    h = h_s[...]
    valid = D_FF - j * tf
    a = jnp.dot(h, wg_ref[...].astype(BF16), preferred_element_type=F32)
    b = jnp.dot(h, wu_ref[...].astype(BF16), preferred_element_type=F32)
    col = lax.broadcasted_iota(jnp.int32, (1, tf), 1)
    act = jnp.where(col < valid, 0.5 * (a * _sigmoid(a) * b), 0.0).astype(BF16)
    row = lax.broadcasted_iota(jnp.int32, (tf, 1), 0)
    wd = jnp.where(row < valid, wd_ref[...], 0.0).astype(BF16)
    o_ref[...] += jnp.dot(act, wd, preferred_element_type=F32)

    if final:
        @pl.when(j == pl.num_programs(1) - 1)
        def _():
            o_ref[...] = _rms_norm(o_ref[...], fg_ref[...])


def _ffn(x, norm_g, wg, wu, wd, layer, final_g=None):
    s, d = x.shape
    tm = min(FFN_ROW_TILE, s)
    tf = FF_TILE
    final = final_g is not None
    in_specs = [
        _resident((tm, d), lambda i, j: (i, 0)),
        pl.BlockSpec((1, d), lambda i, j: (0, 0)),
        pl.BlockSpec((None, d, tf), lambda i, j: (layer, 0, j)),
        pl.BlockSpec((None, d, tf), lambda i, j: (layer, 0, j)),
        pl.BlockSpec((None, tf, d), lambda i, j: (layer, j, 0)),
    ]
    args = [x, norm_g, wg, wu, wd]
    if final:
        in_specs.append(pl.BlockSpec((1, d), lambda i, j: (0, 0)))
        args.append(final_g)
    return pl.pallas_call(
        functools.partial(_ffn_body, final=final),
        grid=(s // tm, pl.cdiv(D_FF, tf)),
        in_specs=in_specs,
        out_specs=pl.BlockSpec((tm, d), lambda i, j: (i, 0)),
        out_shape=jax.ShapeDtypeStruct((s, d), F32),
        scratch_shapes=[pltpu.VMEM((tm, d), BF16)],
        compiler_params=_params(56, "parallel", "arbitrary"),
        name="ffn",
    )(*args)


def _prep_body(wt_ref, main_ref, t_ref, gate_ref):
    w = MLSTM_WIDTH
    b0 = 4 * w + 2 * MLSTM_HEADS

    def piece(first, count):
        return wt_ref[first:first + count, :].T.astype(BF16)

    main_ref[:, 0:w] = piece(0, w)
    main_ref[:, w:3 * w] = piece(2 * w, 2 * w)
    main_ref[:, 3 * w:4 * w] = piece(b0 + w, w)
    main_ref[:, 4 * w:] = piece(b0 + 3 * w, 8 * w)
    t_ref[:, 0:w] = piece(w, w)
    t_ref[:, w:2 * w] = piece(b0, w)
    t_ref[:, 2 * w:] = piece(b0 + 2 * w, w)
    lane = lax.broadcasted_iota(jnp.int32, gate_ref.shape, 1)
    gate_ref[...] = jnp.where(lane < 2 * MLSTM_HEADS, wt_ref[4 * w:4 * w + GATE_PAD, :].T, 0.0).astype(BF16)


def _prep_w_in(w_in):
    depth, d, n_in = w_in.shape
    rk = PREP_ROWS
    return pl.pallas_call(
        _prep_body,
        grid=(depth, d // rk),
        in_specs=[pl.BlockSpec((None, n_in, rk), lambda l, r: (l, 0, r))],
        out_specs=[
            pl.BlockSpec((None, rk, Z_COLS), lambda l, r: (l, r, 0)),
            pl.BlockSpec((None, rk, T_COLS), lambda l, r: (l, r, 0)),
            pl.BlockSpec((None, rk, GATE_PAD), lambda l, r: (l, r, 0)),
        ],
        out_shape=[
            jax.ShapeDtypeStruct((depth, d, Z_COLS), BF16),
            jax.ShapeDtypeStruct((depth, d, T_COLS), BF16),
            jax.ShapeDtypeStruct((depth, d, GATE_PAD), BF16),
        ],
        compiler_params=_params(48, "parallel", "parallel"),
        name="prep_w_in",
    )(jnp.swapaxes(w_in, 1, 2))


def _proj_body(x_ref, g_ref, w_ref, wt_ref, wif_ref,
               z_ref, akt_ref, bqt_ref, bvt_ref, ifc_ref, ifr_ref, h_s):
    @pl.when(pl.program_id(1) == 0)
    def _():
        h_s[...] = _rms_norm(x_ref[...], g_ref[...]).astype(BF16)
        h = h_s[...]
        w = MLSTM_WIDTH
        tm = h.shape[0]
        akt_ref[...] = jnp.dot(h, wt_ref[:, 0:w], preferred_element_type=F32).astype(BF16).T
        bq = jnp.dot(h, wt_ref[:, w:2 * w], preferred_element_type=F32) * MOBA_Q_SCALE
        bqt_ref[...] = bq.astype(BF16).T
        bvt = jnp.dot(h, wt_ref[:, 2 * w:3 * w], preferred_element_type=F32).astype(BF16).T
        ones = jnp.ones((BF16_SUBLANES, tm), BF16)
        dh = MOBA_HEAD_DIM
        bvt = jnp.concatenate(
            [part for hd in range(MOBA_HEADS) for part in (bvt[hd * dh:(hd + 1) * dh, :], ones)], axis=0)
        for n in range(bvt_ref.shape[0]):
            bvt_ref[n] = bvt[:, n * MOBA_BLOCK:(n + 1) * MOBA_BLOCK]
        ifc_ref[...] = jnp.dot(h, wif_ref[...], preferred_element_type=F32)
        ifr_ref[...] = ifc_ref[...].T[0:GATE_ROWS, :]

    z_ref[...] = jnp.dot(h_s[...], w_ref[...], preferred_element_type=F32).astype(BF16)


def _proj(x, norm_g, w_main, w_t, w_if, layer):
    s, d = x.shape
    tm = min(ROW_TILE, s)
    tn = PROJ_COL_TILE
    blk = MOBA_BLOCK
    vrows = MOBA_HEADS * MOBA_VROWS
    return pl.pallas_call(
        _proj_body,
        grid=(s // tm, Z_COLS // tn),
        in_specs=[
            _resident((tm, d), lambda i, j: (i, 0)),
            pl.BlockSpec((1, d), lambda i, j: (0, 0)),
            pl.BlockSpec((None, d, tn), lambda i, j: (layer, 0, j)),
            _resident((None, d, T_COLS), lambda i, j: (layer, 0, 0)),
            pl.BlockSpec((None, d, GATE_PAD), lambda i, j: (layer, 0, 0)),
        ],
        out_specs=[
            pl.BlockSpec((tm, tn), lambda i, j: (i, j)),
            pl.BlockSpec((MLSTM_WIDTH, tm), lambda i, j: (0, i)),
            pl.BlockSpec((MOBA_WIDTH, tm), lambda i, j: (0, i)),
            pl.BlockSpec((tm // blk, vrows, blk), lambda i, j: (i, 0, 0)),
            pl.BlockSpec((tm, GATE_PAD), lambda i, j: (i, 0)),
            pl.BlockSpec((GATE_ROWS, tm), lambda i, j: (0, i)),
        ],
        out_shape=[
            jax.ShapeDtypeStruct((s, Z_COLS), BF16),
            jax.ShapeDtypeStruct((MLSTM_WIDTH, s), BF16),
            jax.ShapeDtypeStruct((MOBA_WIDTH, s), BF16),
            jax.ShapeDtypeStruct((s // blk, vrows, blk), BF16),
            jax.ShapeDtypeStruct((s, GATE_PAD), F32),
            jax.ShapeDtypeStruct((GATE_ROWS, s), F32),
        ],
        scratch_shapes=[pltpu.VMEM((tm, d), BF16)],
        compiler_params=_params(56, "parallel", "arbitrary"),
        name="proj",
    )(x, norm_g, w_main, w_t, w_if)


def _mlstm_body(q_ref, v_ref, o_ref, kt_ref, ifc_ref, ifr_ref, bias_r_ref, bias_c_ref, gain_ref,
                y_ref, ct_s, m_s):
    chunk = q_ref.shape[0]
    dh = MLSTM_HEAD_DIM

    @pl.when(pl.program_id(0) == 0)
    def _():
        ct_s[...] = jnp.zeros_like(ct_s)
        m_s[...] = jnp.zeros_like(m_s)

    gates_c = ifc_ref[...] + bias_r_ref[...]
    gates_r = ifr_ref[...] + bias_c_ref[...]
    row = lax.broadcasted_iota(jnp.int32, (chunk, chunk), 0)
    col = lax.broadcasted_iota(jnp.int32, (chunk, chunk), 1)
    causal = row >= col
    tril = jnp.where(causal, 1.0, 0.0).astype(BF16)
    triu = jnp.where(row <= col, 1.0, 0.0).astype(BF16)
    bcum_c = sum(jnp.dot(tril, p, preferred_element_type=F32) for p in _split3(_log_sigmoid(gates_c)))
    bcum_r = sum(jnp.dot(p, triu, preferred_element_type=F32) for p in _split3(_log_sigmoid(gates_r)))
    ones_col = jnp.where(lax.broadcasted_iota(jnp.int32, (chunk, LANES), 1) == 0, 1.0, 0.0).astype(BF16)
    k_scale = dh ** -0.5

    for hd in range(MLSTM_HEADS):
        sl = slice(hd * dh, (hd + 1) * dh)
        fcol = MLSTM_HEADS + hd
        q = q_ref[:, sl]
        kt = kt_ref[sl, :]
        v_aug = jnp.concatenate([v_ref[:, sl], ones_col], axis=1)
        bc = bcum_c[:, fcol:fcol + 1]
        br = bcum_r[fcol:fcol + 1, :]
        i_r = gates_r[hd:hd + 1, :]
        m_prev = m_s[hd:hd + 1, 0:1]
        ct = ct_s[hd]

        d_intra = jnp.where(causal, bc - br + i_r, -jnp.inf)
        d_inter = bc + m_prev
        m_t = jnp.maximum(jnp.max(d_intra, axis=1, keepdims=True), d_inter)
        w_intra = jnp.exp(d_intra - m_t)
        w_inter = jnp.exp(d_inter - m_t)
        qk = jnp.dot(q, kt, preferred_element_type=F32)
        sc = (qk * k_scale * w_intra).astype(BF16)
        numden = (jnp.dot(sc, v_aug, preferred_element_type=F32)
                  + w_inter * jnp.dot(q, ct.astype(BF16), preferred_element_type=F32))
        num = numden[:, :dh]
        den = numden[:, dh:dh + 1]
        h_out = num / jnp.maximum(jnp.abs(den), jnp.exp(-m_t))
        h_out = h_out * lax.rsqrt(jnp.mean(h_out * h_out, axis=-1, keepdims=True) + EPS)
        h_out = h_out * gain_ref[:, sl]
        y_ref[:, sl] = (_sigmoid(o_ref[:, sl].astype(F32)) * h_out).astype(BF16)

        b_last = bc[chunk - 1:chunk, :]
        g_r = b_last - br + i_r
        m_new = jnp.maximum(b_last + m_prev, jnp.max(g_r, axis=1, keepdims=True))
        decay = jnp.exp(b_last + m_prev - m_new)
        w_r = jnp.exp(g_r - m_new) * k_scale
        ktw = (kt.astype(F32) * w_r).astype(BF16)
        ct_s[hd] = decay * ct + jnp.dot(ktw, v_aug, preferred_element_type=F32)
        m_s[hd:hd + 1, :] = jnp.broadcast_to(m_new, (1, LANES))


def _mlstm(z, kt, ifc, ifr, bias_r, bias_c, gain):
    s = z.shape[0]
    chunk = min(MLSTM_CHUNK, s)
    w = MLSTM_WIDTH
    return pl.pallas_call(
        _mlstm_body,
        grid=(s // chunk,),
        in_specs=[
            pl.BlockSpec((chunk, w), lambda c: (c, Z_AQ)),
            pl.BlockSpec((chunk, w), lambda c: (c, Z_AV)),
            pl.BlockSpec((chunk, w), lambda c: (c, Z_AO)),
            pl.BlockSpec((w, chunk), lambda c: (0, c)),
            pl.BlockSpec((chunk, GATE_PAD), lambda c: (c, 0)),
            pl.BlockSpec((GATE_ROWS, chunk), lambda c: (0, c)),
            pl.BlockSpec((1, GATE_PAD), lambda c: (0, 0)),
            pl.BlockSpec((GATE_ROWS, 1), lambda c: (0, 0)),
            pl.BlockSpec((1, w), lambda c: (0, 0)),
        ],
        out_specs=pl.BlockSpec((chunk, w), lambda c: (c, 0)),
        out_shape=jax.ShapeDtypeStruct((s, w), BF16),
        scratch_shapes=[
            pltpu.VMEM((MLSTM_HEADS, MLSTM_HEAD_DIM, MLSTM_HEAD_DIM + LANES), F32),
            pltpu.VMEM((SUBLANES, LANES), F32),
        ],
        compiler_params=_params(32, "arbitrary"),
        name="mlstm",
    )(z, z, z, kt, ifc, ifr, bias_r, bias_c, gain)


def _moba_body(qt_ref, k_ref, vt_ref, y_ref, kmean_s, qaug_s, acc_s, sc_s):
    blk = MOBA_BLOCK
    dh = MOBA_HEAD_DIM
    vr = MOBA_VROWS
    heads = qt_ref.shape[0] // dh
    n_blk = k_ref.shape[0] // blk
    nbp = kmean_s.shape[0]
    i = pl.program_id(1)

    @pl.when(i == 0)
    def _():
        kmean_s[...] = jnp.zeros_like(kmean_s)

        def mean_block(b, carry):
            kb = k_ref[pl.ds(pl.multiple_of(b * blk, blk), blk), :].astype(F32)
            kmean_s[pl.ds(b, 1), :] = jnp.mean(kb, axis=0, keepdims=True)
            return carry

        lax.fori_loop(0, n_blk, mean_block, 0)

    block_id = lax.broadcasted_iota(jnp.int32, (nbp, blk), 0).astype(F32)
    i_f = i.astype(F32)
    for hd in range(heads):
        hs = slice(hd * dh, (hd + 1) * dh)
        qt = qt_ref[hs, :]
        km = kmean_s[:, hs]
        km_hi = km.astype(BF16)
        km_lo = (km - km_hi.astype(F32)).astype(BF16)
        g = (jnp.dot(km_hi, qt, preferred_element_type=F32)
             + jnp.dot(km_lo, qt, preferred_element_type=F32))
        g = jnp.where(block_id < i_f, g, -jnp.inf)
        bias = jnp.full((nbp, blk), MASK_NEG, F32)
        for _ in range(MOBA_TOPK):
            mx = jnp.max(g, axis=0, keepdims=True)
            first = jnp.min(jnp.where(g == mx, block_id, float(nbp)), axis=0, keepdims=True)
            first = jnp.where(mx > -jnp.inf, first, -1.0)
            pick = block_id == first
            bias = jnp.where(pick, 0.0, bias)
            g = jnp.where(pick, -jnp.inf, g)
        if nbp < LANES:
            bias = jnp.concatenate([bias, jnp.zeros((LANES - nbp, blk), F32)], axis=0)
        qaug_s[hd] = jnp.concatenate([qt, bias.astype(BF16)], axis=0)
        acc_s[hd] = jnp.zeros((vr, blk), F32)

    lane = lax.broadcasted_iota(jnp.int32, (blk, LANES), 1)

    def block_onehot(j):
        return jnp.where(lane == jnp.where(j < i, j, -1), 1.0, 0.0).astype(BF16)

    def block_scores(hd, j, onehot):
        k_j = k_ref[pl.ds(pl.multiple_of(j * blk, blk), blk), hd * dh:(hd + 1) * dh]
        return jnp.dot(jnp.concatenate([k_j, onehot], axis=1), qaug_s[hd], preferred_element_type=F32)

    def attend(hd, m_old, st, j):
        m_new = jnp.maximum(m_old, jnp.max(st, axis=0, keepdims=True))
        alpha = jnp.exp2(m_old - m_new)
        p = jnp.exp2(st - m_new).astype(BF16)
        vt = vt_ref[j, hd * vr:(hd + 1) * vr, :]
        acc_s[hd] = alpha * acc_s[hd] + jnp.dot(vt, p, preferred_element_type=F32)
        return m_new

    first = block_onehot(0)
    for hd in range(heads):
        sc_s[hd] = block_scores(hd, 0, first)

    def past_block(j, maxima):
        onehot = block_onehot(j + 1)
        new_maxima = []
        st = sc_s[0]
        sc_s[0] = block_scores(0, j + 1, onehot)
        for hd in range(heads):
            if hd + 1 < heads:
                st_next = sc_s[hd + 1]
                sc_s[hd + 1] = block_scores(hd + 1, j + 1, onehot)
            new_maxima.append(attend(hd, maxima[hd], st, j))
            st = st_next
        return tuple(new_maxima)

    init = (jnp.full((1, blk), NEG_INIT, F32),) * heads
    group = MOBA_LOOP_BLOCKS

    def past_blocks(jj, maxima):
        for n in range(group):
            maxima = past_block(group * jj + n, maxima)
        return maxima

    maxima = lax.fori_loop(0, i // group, past_blocks, init)
    maxima = lax.fori_loop((i // group) * group, i, past_block, maxima)

    key_pos = lax.broadcasted_iota(jnp.int32, (blk, blk), 0)
    query_pos = lax.broadcasted_iota(jnp.int32, (blk, blk), 1)
    causal = key_pos <= query_pos
    for hd in range(heads):
        attend(hd, maxima[hd], jnp.where(causal, sc_s[hd], -jnp.inf), i)
        out = acc_s[hd, 0:dh, :] / acc_s[hd, dh:dh + 1, :]
        y_ref[:, hd * dh:(hd + 1) * dh] = out.T.astype(BF16)


def _moba(z, bqt, bvt):
    s = z.shape[0]
    blk = MOBA_BLOCK
    dh = MOBA_HEAD_DIM
    gw = MOBA_HEAD_GROUP * dh
    n_blk = s // blk
    nbp = -(-n_blk // SUBLANES) * SUBLANES
    k_tile = Z_BK * (MOBA_WIDTH // gw)
    return pl.pallas_call(
        _moba_body,
        grid=(MOBA_WIDTH // gw, n_blk),
        in_specs=[
            pl.BlockSpec((gw, blk), lambda g, i: (g, i)),
            _resident((s, gw), lambda g, i: (0, k_tile + g)),
            _resident((n_blk, MOBA_HEAD_GROUP * MOBA_VROWS, blk), lambda g, i: (0, g, 0)),
        ],
        out_specs=pl.BlockSpec((blk, gw), lambda g, i: (i, g)),
        out_shape=jax.ShapeDtypeStruct((s, MOBA_WIDTH), BF16),
        scratch_shapes=[
            pltpu.VMEM((nbp, gw), F32),
            pltpu.VMEM((MOBA_HEAD_GROUP, dh + LANES, blk), BF16),
            pltpu.VMEM((MOBA_HEAD_GROUP, MOBA_VROWS, blk), F32),
            pltpu.VMEM((MOBA_HEAD_GROUP, blk, blk), F32),
        ],
        compiler_params=_params(48, "parallel", "arbitrary"),
        name="moba",
    )(bqt, z, bvt)


def _lru_body(x_ref, gate_ref, cw_ref, cb_ref, wa_ref, ba_ref, wx_ref, bx_ref, lam_ref,
              y_ref, tail_s, hlast_s, a_s, u_s):
    t, c = x_ref.shape

    @pl.when(pl.program_id(0) == 0)
    def _():
        tail_s[...] = jnp.zeros_like(tail_s)
        hlast_s[...] = jnp.zeros_like(hlast_s)

    x = x_ref[...].astype(F32)
    xx = jnp.concatenate([tail_s[...], x], axis=0)
    tail_s[...] = x[t - SUBLANES:, :]
    xc = cb_ref[...] + cw_ref[LRU_CONV - 1:LRU_CONV, :] * x
    for tap in range(LRU_CONV - 1):
        shift = LRU_CONV - 1 - tap
        xc = xc + cw_ref[tap:tap + 1, :] * xx[SUBLANES - shift:SUBLANES - shift + t, :]

    xcb = xc.astype(BF16)
    bd = LRU_BLOCK_DIM
    ra = jnp.concatenate(
        [jnp.dot(xcb[:, n * bd:(n + 1) * bd], wa_ref[n], preferred_element_type=F32) for n in range(LRU_BLOCKS)],
        axis=1)
    rx = jnp.concatenate(
        [jnp.dot(xcb[:, n * bd:(n + 1) * bd], wx_ref[n], preferred_element_type=F32) for n in range(LRU_BLOCKS)],
        axis=1)
    r = _sigmoid_tanh(ra + ba_ref[...])
    gi = _sigmoid_tanh(rx + bx_ref[...])
    log_a = -LRU_C * r * _softplus(-lam_ref[...])
    a = jnp.exp(log_a)
    a_s[...] = a
    s = -jnp.tanh(log_a) * (a * a + 1.0)
    root = jnp.where(s > 0.0, s * lax.rsqrt(s), 0.0)
    u_s[...] = root * (gi * xc)

    sub = lax.broadcasted_iota(jnp.int32, (SUBLANES, c), 0)

    def group(gidx, h_prev):
        r0 = pl.multiple_of(gidx * SUBLANES, SUBLANES)
        a = a_s[pl.ds(r0, SUBLANES), :]
        u = u_s[pl.ds(r0, SUBLANES), :]
        d = 1
        while d < SUBLANES:
            keep = sub >= d
            a_sh = jnp.where(keep, pltpu.roll(a, d, 0), 1.0)
            u_sh = jnp.where(keep, pltpu.roll(u, d, 0), 0.0)
            u = a * u_sh + u
            a = a * a_sh
            d *= 2
        h = u + a * h_prev
        u_s[pl.ds(r0, SUBLANES), :] = h
        return jnp.broadcast_to(h[SUBLANES - 1:SUBLANES, :], (SUBLANES, c))

    h_last = lax.fori_loop(0, t // SUBLANES, group, hlast_s[...], unroll=4)
    hlast_s[...] = h_last

    gt = gate_ref[...].astype(F32)
    gelu = 0.5 * gt * (1.0 + jnp.tanh(0.7978845608028654 * (gt + 0.044715 * (gt * gt * gt))))
    y_ref[...] = (u_s[...] * gelu).astype(BF16)


def _lru(z, conv_w, conv_b, w_a, b_a, w_x, b_x, lam):
    s = z.shape[0]
    c = LRU_WIDTH
    t = min(LRU_TILE, s)
    bd = LRU_BLOCK_DIM
    vec = pl.BlockSpec((1, c), lambda i: (0, 0))
    blockw = pl.BlockSpec((LRU_BLOCKS, bd, bd), lambda i: (0, 0, 0))
    return pl.pallas_call(
        _lru_body,
        grid=(s // t,),
        in_specs=[
            pl.BlockSpec((t, c), lambda i: (i, Z_CX)),
            pl.BlockSpec((t, c), lambda i: (i, Z_CG)),
            pl.BlockSpec((LRU_CONV, c), lambda i: (0, 0)),
            vec, blockw, vec, blockw, vec, vec,
        ],
        out_specs=pl.BlockSpec((t, c), lambda i: (i, 0)),
        out_shape=jax.ShapeDtypeStruct((s, c), BF16),
        scratch_shapes=[
            pltpu.VMEM((SUBLANES, c), F32),
            pltpu.VMEM((SUBLANES, c), F32),
            pltpu.VMEM((t, c), F32),
            pltpu.VMEM((t, c), F32),
        ],
        compiler_params=_params(32, "arbitrary"),
        name="rglru",
    )(z, z, conv_w, conv_b, w_a, b_a, w_x, b_x, lam)


def _merge_body(x_ref, ya_ref, yb_ref, yc_ref, ga_ref, gb_ref, gc_ref, wa_ref, wb_ref, wc_ref, wo_ref, o_ref):
    @pl.when(pl.program_id(1) == 0)
    def _():
        o_ref[...] = x_ref[...]

    half = ga_ref.shape[1] // 2
    parts = []
    for c in range(2):
        cols = slice(c * half, (c + 1) * half)
        merged = (_sigmoid(ga_ref[:, cols].astype(F32)) * jnp.dot(ya_ref[...], wa_ref[:, cols], preferred_element_type=F32)
                  + _sigmoid(gb_ref[:, cols].astype(F32)) * jnp.dot(yb_ref[...], wb_ref[:, cols], preferred_element_type=F32)
                  + _sigmoid(gc_ref[:, cols].astype(F32)) * jnp.dot(yc_ref[...], wc_ref[:, cols], preferred_element_type=F32))
        parts.append(merged.astype(BF16))
    acc = o_ref[...]
    for c in range(2):
        acc = acc + jnp.dot(parts[c], wo_ref[c * half:(c + 1) * half, :], preferred_element_type=F32)
    o_ref[...] = acc


def _merge(x, ya, yb, yc, z, wa, wb, wc, wo, layer):
    s, d = x.shape
    tm = min(ROW_TILE, s)
    tn = MERGE_COL_TILE
    w = ya.shape[1]
    per_tile = MLSTM_WIDTH // tn
    ybranch = pl.BlockSpec((tm, w), lambda i, j: (i, 0))
    wup = pl.BlockSpec((None, w, tn), lambda i, j: (layer, 0, j))

    def gate_spec(base):
        return pl.BlockSpec((tm, tn), lambda i, j: (i, base * per_tile + j))

    return pl.pallas_call(
        _merge_body,
        grid=(s // tm, d // tn),
        in_specs=[
            pl.BlockSpec((tm, d), lambda i, j: (i, 0)),
            ybranch, ybranch, ybranch,
            gate_spec(Z_GA), gate_spec(Z_GB), gate_spec(Z_GC),
            wup, wup, wup,
            pl.BlockSpec((None, tn, d), lambda i, j: (layer, j, 0)),
        ],
        out_specs=pl.BlockSpec((tm, d), lambda i, j: (i, 0)),
        out_shape=jax.ShapeDtypeStruct((s, d), F32),
        compiler_params=_params(48, "parallel", "arbitrary"),
        name="merge",
    )(x, ya, yb, yc, z, z, z, wa, wb, wc, wo)


def _row(v):
    return v.reshape(1, -1).astype(F32)


def kernel(x, ffn1_norm, ffn1_w_gate, ffn1_w_up, ffn1_w_down, mix_norm, w_in, mlstm_b_i, mlstm_b_f, mlstm_head_norm, lru_conv_w, lru_conv_b, lru_w_a, lru_b_a, lru_w_x, lru_b_x, lru_lambda, w_up_a, w_up_b, w_up_c, w_out, ffn2_norm, ffn2_w_gate, ffn2_w_up, ffn2_w_down, final_norm):
    b, s, d = x.shape
    depth = w_in.shape[0]
    w_main, w_t, w_if = _prep_w_in(w_in)
    merge_w = tuple(w.astype(BF16) for w in (w_up_a, w_up_b, w_up_c, w_out))
    outs = []
    for bi in range(b):
        xs = x[bi]
        for l in range(depth):
            xs = _ffn(xs, _row(ffn1_norm[l]), ffn1_w_gate, ffn1_w_up, ffn1_w_down, l)
            z, akt, bqt, bvt, ifc, ifr = _proj(xs, _row(mix_norm[l]), w_main, w_t, w_if, l)
            gate_bias = jnp.concatenate([mlstm_b_i[l], mlstm_b_f[l]]).astype(F32)
            bias_r = jnp.pad(gate_bias, (0, GATE_PAD - gate_bias.shape[0])).reshape(1, GATE_PAD)
            bias_c = jnp.pad(gate_bias, (0, GATE_ROWS - gate_bias.shape[0])).reshape(GATE_ROWS, 1)
            ya = _mlstm(z, akt, ifc, ifr, bias_r, bias_c, _row(mlstm_head_norm[l]))
            yb = _moba(z, bqt, bvt)
            yc = _lru(z, lru_conv_w[l].astype(F32), _row(lru_conv_b[l]), lru_w_a[l].astype(BF16), _row(lru_b_a[l]),
                      lru_w_x[l].astype(BF16), _row(lru_b_x[l]), _row(lru_lambda[l]))
            xs = _merge(xs, ya, yb, yc, z, *merge_w, l)
            xs = _ffn(xs, _row(ffn2_norm[l]), ffn2_w_gate, ffn2_w_up, ffn2_w_down, l,
                      final_g=_row(final_norm) if l == depth - 1 else None)
        outs.append(xs)
    return jnp.stack(outs).astype(x.dtype)
```
